```python
import jax, jax.numpy as jnp
from jax import lax
import numpy as np

D_MODEL = 2048
BATCH = 4
SEQ = 2048
DEPTH = 4

GRID_W = 64
CTX_LEN = 256
N_MIXERS = 3
N_RET = (DEPTH + 2) // 3
N_GMLP = (DEPTH + 1) // 3
N_CONV = DEPTH // 3

D_FF = 5632
FFN_RES_WEIGHT = 0.5

RET_HEADS = 8
RET_DK = D_MODEL // RET_HEADS
RET_DV = 2 * RET_DK
RET_QK = RET_HEADS * RET_DK
RET_V = RET_HEADS * RET_DV
RET_IN = 2 * RET_QK + 2 * RET_V
RET_CHUNK = 128
ROPE_BASE = 10000.0

GMLP_CHUNK = 128
GMLP_GROUPS = 8
GMLP_E = 3 * D_MODEL

CONV_K = 31
EPS = 1e-6

kernel_name = 'hybrid_retention_gmlp_conformer_macaron'


def standardize(x):
    x32 = x.astype(jnp.float32)
    xc = x32 - jnp.mean(x32, -1, keepdims=True)
    return xc * lax.rsqrt(jnp.mean(xc * xc, -1, keepdims=True) + EPS)


def layer_norm(x, g, b):
    return (standardize(x) * g.astype(jnp.float32) + b.astype(jnp.float32)).astype(x.dtype)


def rms_norm(x, g):
    x32 = x.astype(jnp.float32)
    y = x32 * lax.rsqrt(jnp.mean(x32 * x32, -1, keepdims=True) + EPS)
    return (y * g.astype(jnp.float32)).astype(x.dtype)


def modulate_in(t, g_pre, shift, scale):
    return rms_norm(t, g_pre) * (1 + scale) + shift


def gated_residual(t, y, g_post, gate, weight):
    return t + weight * gate * rms_norm(y, g_post)


def swiglu(h, w_in, w_out):
    a, b = jnp.split(h @ w_in, 2, axis=-1)
    return (jax.nn.silu(a) * b) @ w_out


def ffn_sublayer(t, m, g_pre, g_post, w_in, w_out, s):
    h = modulate_in(t, g_pre, m[:, :, 3 * s], m[:, :, 3 * s + 1])
    return gated_residual(t, swiglu(h, w_in, w_out), g_post, m[:, :, 3 * s + 2], FFN_RES_WEIGHT)


def rope_1d(x, pos):
    half = x.shape[-1] // 2
    inv = ROPE_BASE ** (-jnp.arange(half, dtype=jnp.float32) / half)
    ang = pos[:, None] * inv[None, :]
    cos = jnp.cos(ang).astype(x.dtype)
    sin = jnp.sin(ang).astype(x.dtype)
    x1, x2 = x[..., :half], x[..., half:]
    return jnp.concatenate([x1 * cos - x2 * sin, x1 * sin + x2 * cos], axis=-1)


def axial_rope(x, rows, cols):
    half = x.shape[-1] // 2
    return jnp.concatenate([rope_1d(x[..., :half], rows), rope_1d(x[..., half:], cols)], axis=-1)


def to_heads(t, d):
    b, l, _ = t.shape
    return t.reshape(b, l, -1, d).transpose(0, 2, 1, 3)


def ret_scan(q, k, v, log_g, s0, include_diag):
    b, h, l, _ = q.shape
    dv = v.shape[-1]
    n = l // RET_CHUNK

    def chunks(t):
        return t.reshape(b, h, n, RET_CHUNK, t.shape[-1]).transpose(2, 0, 1, 3, 4)

    pos = jnp.arange(RET_CHUNK, dtype=jnp.float32)
    diff = pos[:, None] - pos[None, :]
    mask = (diff >= (0.0 if include_diag else 1.0))[None]
    d_in = jnp.where(mask, jnp.exp(jnp.where(mask, diff[None], 0.0) * log_g[:, None, None]), 0.0).astype(q.dtype)
    q_dec = jnp.exp((pos + 1.0)[None, :] * log_g[:, None])[None, :, :, None]
    k_dec = jnp.exp((RET_CHUNK - 1.0 - pos)[None, :] * log_g[:, None])[None, :, :, None].astype(k.dtype)
    c_dec = jnp.exp(RET_CHUNK * log_g)[None, :, None, None]

    def step(s, qkv):
        qc, kc, vc = qkv
        a = jnp.einsum('bhnd,bhmd->bhnm', qc, kc) * d_in
        o = (jnp.einsum('bhnm,bhme->bhne', a, vc).astype(jnp.float32)
             + jnp.einsum('bhnd,bhde->bhne', qc.astype(jnp.float32), s) * q_dec)
        s = s * c_dec + jnp.einsum('bhmd,bhme->bhde', kc * k_dec, vc).astype(jnp.float32)
        return s, o

    s, o = lax.scan(step, s0, (chunks(q), chunks(k), chunks(v)))
    o = o.transpose(1, 2, 0, 3, 4).reshape(b, h, l, dv).astype(q.dtype)
    return o, s


def ret_final_state(k, v, log_g, reverse):
    l = k.shape[2]
    pos = jnp.arange(l, dtype=jnp.float32)
    expo = pos if reverse else (l - 1.0 - pos)
    w = jnp.exp(expo[None, :] * log_g[:, None])[None, :, :, None].astype(k.dtype)
    return jnp.einsum('bhld,bhle->bhde', k * w, v).astype(jnp.float32)


def ret_output(o, g, gn_g, w_out):
    b, _, l, _ = o.shape
    on = standardize(o.transpose(0, 2, 1, 3)).reshape(b, l, RET_V) * gn_g.astype(jnp.float32)
    return (jax.nn.silu(g) * on.astype(g.dtype)) @ w_out


def retention_mixer(h_lat, h_ctx, w_in, w_out, decay_logit, gn_g, rows, cols, ctx_out):
    log_g = jax.nn.log_sigmoid(decay_logit.astype(jnp.float32))
    k_scale = RET_DK ** -0.5
    p = h_lat @ w_in
    q = axial_rope(to_heads(p[..., :RET_QK], RET_DK), rows, cols)
    k = axial_rope(to_heads(p[..., RET_QK:2 * RET_QK], RET_DK), rows, cols) * k_scale
    v = to_heads(p[..., 2 * RET_QK:2 * RET_QK + RET_V], RET_DV)
    g = p[..., 2 * RET_QK + RET_V:]
    bsz = h_ctx.shape[0]
    if ctx_out:
        pc = h_ctx @ w_in
        qc = to_heads(pc[..., :RET_QK], RET_DK)
        kc = to_heads(pc[..., RET_QK:2 * RET_QK], RET_DK) * k_scale
        vc = to_heads(pc[..., 2 * RET_QK:2 * RET_QK + RET_V], RET_DV)
        gc = pc[..., 2 * RET_QK + RET_V:]
        s0 = jnp.zeros((bsz, RET_HEADS, RET_DK, RET_DV), jnp.float32)
        oc_f, s_f = ret_scan(qc, kc, vc, log_g[0], s0, True)
        oc_b, s_b = ret_scan(jnp.flip(qc, 2), jnp.flip(kc, 2), jnp.flip(vc, 2), log_g[1], s0, False)
        y_ctx = ret_output(oc_f + jnp.flip(oc_b, 2), gc, gn_g, w_out)
    else:
        kc = to_heads(h_ctx @ w_in[:, RET_QK:2 * RET_QK], RET_DK) * k_scale
        vc = to_heads(h_ctx @ w_in[:, 2 * RET_QK:2 * RET_QK + RET_V], RET_DV)
        s_f = ret_final_state(kc, vc, log_g[0], False)
        s_b = ret_final_state(kc, vc, log_g[1], True)
        y_ctx = None
    o_f, _ = ret_scan(q, k, v, log_g[0], s_f, True)
    o_b, _ = ret_scan(jnp.flip(q, 2), jnp.flip(k, 2), jnp.flip(v, 2), log_g[1], s_b, False)
    y_lat = ret_output(o_f + jnp.flip(o_b, 2), g, gn_g, w_out)
    return y_lat, y_ctx


def gmlp_mixer(h, w_in, ln_g, ln_b, w_s, b_s, w_out):
    z = jax.nn.gelu(h @ w_in)
    u, v = z[..., :GMLP_E], z[..., GMLP_E:]
    v = layer_norm(v, ln_g, ln_b)
    b, l, _ = v.shape
    v = v.reshape(b, l // GMLP_CHUNK, GMLP_CHUNK, GMLP_GROUPS, GMLP_E // GMLP_GROUPS)
    v = jnp.einsum('gnm,bkmge->bknge', w_s, v) + b_s.T[None, None, :, :, None]
    return (u * v.reshape(b, l, GMLP_E)) @ w_out


def conv_mixer(h, w_pw1, w_dw, b_dw, ln_g, ln_b, w_pw2):
    a, b = jnp.split(h @ w_pw1, 2, axis=-1)
    z = a * jax.nn.sigmoid(b)
    z = lax.conv_general_dilated(z, w_dw[:, None, :], (1,), [(CONV_K // 2, CONV_K // 2)],
                                 dimension_numbers=('NWC', 'WIO', 'NWC'),
                                 feature_group_count=D_MODEL) + b_dw
    z = jax.nn.silu(layer_norm(z, ln_g, ln_b))
    return z @ w_pw2


def setup_inputs(seed: int = 0) -> dict:
    key = jax.random.key(seed)
    ks = jax.random.split(key, 32)
    f32 = jnp.float32

    def nrm(k, shape, scale):
        return jax.random.normal(k, shape, f32) * scale

    base = 1.0 - 2.0 ** (-5.0 - jnp.arange(RET_HEADS, dtype=f32))
    logit = jnp.log(base) - jnp.log1p(-base)
    return {
        'x': nrm(ks[0], (BATCH, SEQ, D_MODEL), 1.0),
        'c': nrm(ks[1], (BATCH, D_MODEL), 1.0),
        'ctx': nrm(ks[2], (BATCH, CTX_LEN, D_MODEL), 1.0),
        'c_ctx': nrm(ks[3], (D_MODEL,), 1.0),
        'ada_w': nrm(ks[4], (DEPTH, D_MODEL, 9 * D_MODEL), D_MODEL ** -0.5),
        'ada_b': nrm(ks[5], (DEPTH, 9 * D_MODEL), 0.02),
        'norm_g': 1.0 + nrm(ks[6], (DEPTH, 3, 2, D_MODEL), 0.02),
        'ffn_w_in': nrm(ks[7], (DEPTH, 2, D_MODEL, 2 * D_FF), D_MODEL ** -0.5),
        'ffn_w_out': nrm(ks[8], (DEPTH, 2, D_FF, D_MODEL), D_FF ** -0.5),
        'ret_w_in': nrm(ks[9], (N_RET, D_MODEL, RET_IN), D_MODEL ** -0.5),
        'ret_w_out': nrm(ks[10], (N_RET, RET_V, D_MODEL), RET_V ** -0.5),
        'ret_decay_logit': logit + nrm(ks[11], (N_RET, 2, RET_HEADS), 0.1),
        'ret_gn_g': 1.0 + nrm(ks[12], (N_RET, RET_V), 0.02),
        'gmlp_w_in': nrm(ks[13], (N_GMLP, D_MODEL, 2 * GMLP_E), D_MODEL ** -0.5),
        'gmlp_ln_g': 1.0 + nrm(ks[14], (N_GMLP, GMLP_E), 0.02),
        'gmlp_ln_b': nrm(ks[15], (N_GMLP, GMLP_E), 0.02),
        'gmlp_w_s': nrm(ks[16], (N_GMLP, GMLP_GROUPS, GMLP_CHUNK, GMLP_CHUNK), GMLP_CHUNK ** -0.5),
        'gmlp_b_s': 1.0 + nrm(ks[17], (N_GMLP, GMLP_GROUPS, GMLP_CHUNK), 0.1),
        'gmlp_w_out': nrm(ks[18], (N_GMLP, GMLP_E, D_MODEL), GMLP_E ** -0.5),
        'conv_w_pw1': nrm(ks[19], (N_CONV, D_MODEL, 2 * D_MODEL), D_MODEL ** -0.5),
        'conv_w_dw': nrm(ks[20], (N_CONV, CONV_K, D_MODEL), CONV_K ** -0.5),
        'conv_b_dw': nrm(ks[21], (N_CONV, D_MODEL), 0.02),
        'conv_ln_g': 1.0 + nrm(ks[22], (N_CONV, D_MODEL), 0.02),
        'conv_ln_b': nrm(ks[23], (N_CONV, D_MODEL), 0.02),
        'conv_w_pw2': nrm(ks[24], (N_CONV, D_MODEL, D_MODEL), D_MODEL ** -0.5),
    }


def reference(x, c, ctx, c_ctx, ada_w, ada_b, norm_g, ffn_w_in, ffn_w_out, ret_w_in, ret_w_out,
              ret_decay_logit, ret_gn_g, gmlp_w_in, gmlp_ln_g, gmlp_ln_b, gmlp_w_s, gmlp_b_s, gmlp_w_out,
              conv_w_pw1, conv_w_dw, conv_b_dw, conv_ln_g, conv_ln_b, conv_w_pw2):
    n_tok = x.shape[1]
    rows_n = n_tok // GRID_W
    rows = jnp.repeat(jnp.arange(rows_n, dtype=jnp.float32), GRID_W)
    cols = jnp.tile(jnp.arange(GRID_W, dtype=jnp.float32), rows_n)
    sc = jax.nn.silu(c)
    sctx = jax.nn.silu(c_ctx)
    for i in range(DEPTH):
        kind = i % N_MIXERS
        inst = i // N_MIXERS
        last = i == DEPTH - 1
        ctx_out = not last
        ctx_needed = ctx_out or kind == 0
        m_lat = (sc @ ada_w[i] + ada_b[i]).reshape(sc.shape[0], 1, 9, D_MODEL)
        m_ctx = (sctx @ ada_w[i] + ada_b[i]).reshape(1, 1, 9, D_MODEL)
        g = norm_g[i]
        x = ffn_sublayer(x, m_lat, g[0, 0], g[0, 1], ffn_w_in[i, 0], ffn_w_out[i, 0], 0)
        if ctx_needed:
            ctx = ffn_sublayer(ctx, m_ctx, g[0, 0], g[0, 1], ffn_w_in[i, 0], ffn_w_out[i, 0], 0)
        h_lat = modulate_in(x, g[1, 0], m_lat[:, :, 3], m_lat[:, :, 4])
        h_ctx = modulate_in(ctx, g[1, 0], m_ctx[:, :, 3], m_ctx[:, :, 4]) if ctx_needed else None
        if kind == 0:
            y_lat, y_ctx = retention_mixer(h_lat, h_ctx, ret_w_in[inst], ret_w_out[inst],
                                           ret_decay_logit[inst], ret_gn_g[inst], rows, cols, ctx_out)
        elif kind == 1:
            gm = (gmlp_w_in[inst], gmlp_ln_g[inst], gmlp_ln_b[inst], gmlp_w_s[inst], gmlp_b_s[inst], gmlp_w_out[inst])
            y_lat = gmlp_mixer(h_lat, *gm)
            y_ctx = gmlp_mixer(h_ctx, *gm) if ctx_out else None
        else:
            cm = (conv_w_pw1[inst], conv_w_dw[inst], conv_b_dw[inst], conv_ln_g[inst], conv_ln_b[inst], conv_w_pw2[inst])
            y_lat = conv_mixer(h_lat, *cm)
            y_ctx = conv_mixer(h_ctx, *cm) if ctx_out else None
        x = gated_residual(x, y_lat, g[1, 1], m_lat[:, :, 5], 1.0)
        if ctx_out:
            ctx = gated_residual(ctx, y_ctx, g[1, 1], m_ctx[:, :, 5], 1.0)
        x = ffn_sublayer(x, m_lat, g[2, 0], g[2, 1], ffn_w_in[i, 1], ffn_w_out[i, 1], 2)
        if ctx_out:
            ctx = ffn_sublayer(ctx, m_ctx, g[2, 0], g[2, 1], ffn_w_in[i, 1], ffn_w_out[i, 1], 2)
    return x
```

```python
import functools
from typing import NamedTuple

import jax
import jax.numpy as jnp
from jax import lax
from jax.experimental import pallas as pl
from jax.experimental.pallas import tpu as pltpu

F32 = jnp.float32
BF16 = jnp.bfloat16

D_MODEL = 2048
DEPTH = 4
GRID_W = 64
N_MIXERS = 3
D_FF = 5632
RET_HEADS = 8
RET_DK = D_MODEL // RET_HEADS
RET_DV = 2 * RET_DK
RET_QK = RET_HEADS * RET_DK
RET_V = RET_HEADS * RET_DV
RET_IN = 2 * RET_QK + 2 * RET_V
CHUNK = 128
ROPE_BASE = 10000.0
GMLP_GROUPS = 8
GMLP_E = 3 * D_MODEL
GMLP_GE = GMLP_E // GMLP_GROUPS
CONV_K = 31
CONV_HALO = 16
EPS = 1e-6
ADA_ROWS = 8
LANES = 128

VMEM_LIMIT = 56 * 1024 * 1024


class _Cfg(NamedTuple):
    batch: int
    seq: int
    ctx: int

    @property
    def n_lat(self):
        return self.batch * self.seq

    @property
    def n_ctx(self):
        return self.batch * self.ctx

    @property
    def n_tok(self):
        return self.n_lat + self.n_ctx


def _params(*sem):
    return pltpu.CompilerParams(dimension_semantics=sem, vmem_limit_bytes=VMEM_LIMIT)


def _tile(cfg, want):
    tm = want
    while cfg.seq % tm or cfg.n_ctx % tm:
        tm //= 2
    return tm


def _mod_row(cfg, tm):
    lat_tiles = cfg.n_lat // tm
    per_seq = cfg.seq // tm
    return lambda i: jnp.where(i < lat_tiles, i // per_seq, cfg.batch)


def _rms(x):
    return x * lax.rsqrt(jnp.mean(x * x, axis=-1, keepdims=True) + EPS)


def _standardize(x):
    xc = x - jnp.mean(x, axis=-1, keepdims=True)
    return xc * lax.rsqrt(jnp.mean(xc * xc, axis=-1, keepdims=True) + EPS)


def _modulate_in(x, g_pre, shift, scale):
    return (_rms(x) * g_pre) * (1.0 + scale) + shift


def _gated_residual(x, y, g_post, gate, weight):
    return x + weight * gate * (_rms(y) * g_post)


def _silu(x):
    return x * jax.nn.sigmoid(x)


def _gelu_tanh(x):
    return 0.5 * x * (1.0 + jnp.tanh(0.7978845608028654 * (x + 0.044715 * (x * x * x))))


def _dot(a, b):
    return jnp.dot(a, b, preferred_element_type=F32)


def _ada_body(c_ref, w_ref, b_ref, o_ref):
    s = _silu(c_ref[...]).astype(BF16)
    o_ref[...] = _dot(s, w_ref[...].astype(BF16)) + b_ref[...]


def _ada_table(cc, ada_w, ada_b):
    depth, d, n = ada_w.shape
    tn = 1024
    out = pl.pallas_call(
        _ada_body,
        grid=(depth, n // tn),
        in_specs=[
            pl.BlockSpec((ADA_ROWS, d), lambda i, j: (0, 0)),
            pl.BlockSpec((None, d, tn), lambda i, j: (i, 0, j)),
            pl.BlockSpec((None, 1, tn), lambda i, j: (i, 0, j)),
        ],
        out_specs=pl.BlockSpec((None, ADA_ROWS, tn), lambda i, j: (i, 0, j)),
        out_shape=jax.ShapeDtypeStruct((depth, ADA_ROWS, n), F32),
        compiler_params=_params("arbitrary", "arbitrary"),
        name="ada_table",
    )(cc, ada_w, ada_b.reshape(depth, 1, n))
    return out.reshape(depth, ADA_ROWS, n // d, d)


def _ffn_body(x_ref, mod_ref, g_ref, wa_ref, wb_ref, wo_ref, o_ref, h_ref, *, sub, n_ff):
    j = pl.program_id(1)

    @pl.when(j == 0)
    def _():
        h = _modulate_in(x_ref[...], g_ref[2 * sub:2 * sub + 1, :],
                         mod_ref[3 * sub:3 * sub + 1, :], mod_ref[3 * sub + 1:3 * sub + 2, :])
        h_ref[...] = h.astype(BF16)
        o_ref[...] = jnp.zeros_like(o_ref)

    h = h_ref[...]
    a = _dot(h, wa_ref[...])
    b = _dot(h, wb_ref[...])
    act = (_silu(a) * b).astype(BF16)
    o_ref[...] += _dot(act, wo_ref[...])

    @pl.when(j == n_ff - 1)
    def _():
        o_ref[...] = _gated_residual(x_ref[...], o_ref[...], g_ref[2 * sub + 1:2 * sub + 2, :],
                                     mod_ref[3 * sub + 2:3 * sub + 3, :], 0.5)


def _ffn_sublayer(cfg, t, mods, g6, w_in, w_out, layer, which, sub, n_rows, tm=512, fc=512):
    d = t.shape[1]
    tm = _tile(cfg, tm)
    f = w_out.shape[2]
    n_ff = f // fc
    row = _mod_row(cfg, tm)
    return pl.pallas_call(
        functools.partial(_ffn_body, sub=sub, n_ff=n_ff),
        grid=(n_rows // tm, n_ff),
        in_specs=[
            pl.BlockSpec((tm, d), lambda i, j: (i, 0)),
            pl.BlockSpec((None, 9, d), lambda i, j: (row(i), 0, 0)),
            pl.BlockSpec((6, d), lambda i, j: (0, 0)),
            pl.BlockSpec((None, None, d, fc), lambda i, j: (layer, which, 0, j)),
            pl.BlockSpec((None, None, d, fc), lambda i, j: (layer, which, 0, n_ff + j)),
            pl.BlockSpec((None, None, fc, d), lambda i, j: (layer, which, j, 0)),
        ],
        out_specs=pl.BlockSpec((tm, d), lambda i, j: (i, 0)),
        out_shape=jax.ShapeDtypeStruct((n_rows, d), F32),
        scratch_shapes=[pltpu.VMEM((tm, d), BF16)],
        compiler_params=_params("arbitrary", "arbitrary"),
        name="ffn_sublayer",
    )(t, mods, g6, w_in, w_in, w_out)


def _inproj_prologue(x_ref, mod_ref, g_ref, h_ref):
    @pl.when(pl.program_id(1) == 0)
    def _():
        h = _modulate_in(x_ref[...], g_ref[2:3, :], mod_ref[3:4, :], mod_ref[4:5, :])
        h_ref[...] = h.astype(BF16)


def _inproj_ret_body(x_ref, mod_ref, g_ref, cos_ref, sin_ref, w_ref, o_ref, h_ref, *, n_rope):
    _inproj_prologue(x_ref, mod_ref, g_ref, h_ref)
    j = pl.program_id(1)
    acc = _dot(h_ref[...], w_ref[...])

    @pl.when(j < n_rope)
    def _():
        half = LANES // 2
        swapped = jnp.concatenate(
            [pltpu.roll(acc[:, g * LANES:(g + 1) * LANES], half, 1) for g in range(RET_DK // LANES)], axis=1)
        r = acc * cos_ref[...] + swapped * sin_ref[...]
        r = r * jnp.where(j >= n_rope // 2, RET_DK ** -0.5, 1.0)
        o_ref[...] = r.astype(BF16)

    @pl.when(j >= n_rope)
    def _():
        o_ref[...] = acc.astype(BF16)


def _inproj_gelu_body(x_ref, mod_ref, g_ref, w_ref, o_ref, h_ref):
    _inproj_prologue(x_ref, mod_ref, g_ref, h_ref)
    o_ref[...] = _gelu_tanh(_dot(h_ref[...], w_ref[...])).astype(BF16)


def _inproj_glu_body(x_ref, mod_ref, g_ref, wa_ref, wb_ref, o_ref, h_ref):
    _inproj_prologue(x_ref, mod_ref, g_ref, h_ref)
    h = h_ref[...]
    o_ref[...] = (_dot(h, wa_ref[...]) * jax.nn.sigmoid(_dot(h, wb_ref[...]))).astype(BF16)


def _inproj(cfg, kind, t, mods, g6, w, rope=None, tm=1024):
    n_tok, d = t.shape
    tm = _tile(cfg, tm)
    row = _mod_row(cfg, tm)
    common = [
        pl.BlockSpec((tm, d), lambda i, j: (i, 0)),
        pl.BlockSpec((None, 9, d), lambda i, j: (row(i), 0, 0)),
        pl.BlockSpec((6, d), lambda i, j: (0, 0)),
    ]
    if kind == "ret":
        tn = RET_DK
        n_out = w.shape[1]
        lat_tiles = cfg.n_lat // tm
        per_seq = cfg.seq // tm
        trow = lambda i: jnp.where(i < lat_tiles, i % per_seq, per_seq + i - lat_tiles)
        body = functools.partial(_inproj_ret_body, n_rope=2 * RET_QK // tn)
        specs = common + [
            pl.BlockSpec((tm, tn), lambda i, j: (trow(i), 0)),
            pl.BlockSpec((tm, tn), lambda i, j: (trow(i), 0)),
            pl.BlockSpec((d, tn), lambda i, j: (0, j)),
        ]
        args = (t, mods, g6, rope[0], rope[1], w)
    elif kind == "gelu":
        tn = 512
        n_out = w.shape[1]
        body = _inproj_gelu_body
        specs = common + [pl.BlockSpec((d, tn), lambda i, j: (0, j))]
        args = (t, mods, g6, w)
    else:
        tn = 512
        n_out = w.shape[1] // 2
        nb = n_out // tn
        body = _inproj_glu_body
        specs = common + [pl.BlockSpec((d, tn), lambda i, j: (0, j)),
                          pl.BlockSpec((d, tn), lambda i, j: (0, nb + j))]
        args = (t, mods, g6, w, w)
    return pl.pallas_call(
        body,
        grid=(n_tok // tm, n_out // tn),
        in_specs=specs,
        out_specs=pl.BlockSpec((tm, tn), lambda i, j: (i, j)),
        out_shape=jax.ShapeDtypeStruct((n_tok, n_out), BF16),
        scratch_shapes=[pltpu.VMEM((tm, d), BF16)],
        compiler_params=_params("arbitrary", "arbitrary"),
        name="inproj_" + kind,
    )(*args)


RET_TILE = 2 * CHUNK


def _ret_body(dec_ref, gn_ref, ql_ref, kl_ref, vl_ref, qc_ref, kc_ref, vc_ref, g_ref, o_ref,
              sf_ref, sb_ref, ob_ref, *, n_lat_chunks, n_ctx_chunks):
    t = pl.program_id(2)
    c = CHUNK
    lgf = dec_ref[0:1, :]
    lgb = dec_ref[1:2, :]
    diff = (lax.broadcasted_iota(jnp.int32, (c, c), 0) - lax.broadcasted_iota(jnp.int32, (c, c), 1)).astype(F32)
    dmat = jnp.where(diff >= 0.0, jnp.exp(jnp.maximum(diff, 0.0) * lgf), jnp.exp(jnp.maximum(-diff, 0.0) * lgb))
    pos = lax.broadcasted_iota(jnp.int32, (c, 1), 0).astype(F32)
    lgf1 = lgf[:, 0:1]
    lgb1 = lgb[:, 0:1]
    qdec_f = jnp.exp((pos + 1.0) * lgf1)
    kdec_f = jnp.exp((c - 1.0 - pos) * lgf1)
    cdec_f = jnp.exp(c * lgf1)
    qdec_b = jnp.exp((c - pos) * lgb1)
    kdec_b = jnp.exp(pos * lgb1)
    cdec_b = jnp.exp(c * lgb1)

    def state_update(s_ref, k, v, kdec, cdec):
        kd = (k.astype(F32) * kdec).T.astype(BF16)
        s_ref[...] = s_ref[...] * cdec + _dot(kd, v)

    def bwd_chunk(q, k, v):
        ob = _dot(q, sb_ref[...].astype(BF16)) * qdec_b
        state_update(sb_ref, k, v, kdec_b, cdec_b)
        return ob

    def fwd_chunk(q, k, v, g, ob):
        a = lax.dot_general(q, k, (((1,), (1,)), ((), ())), preferred_element_type=F32) * dmat
        o = _dot(a.astype(BF16), v) + _dot(q, sf_ref[...].astype(BF16)) * qdec_f + ob
        state_update(sf_ref, k, v, kdec_f, cdec_f)
        return (_silu(g.astype(F32)) * (_standardize(o) * gn_ref[...])).astype(BF16)

    def rows(i):
        return pl.ds(pl.multiple_of(i * c, c), c)

    @pl.when(t == 0)
    def _():
        sf_ref[...] = jnp.zeros_like(sf_ref)
        sb_ref[...] = jnp.zeros_like(sb_ref)
        for i in reversed(range(n_ctx_chunks)):
            sl = slice(i * c, (i + 1) * c)
            ob_ref[sl, :] = bwd_chunk(qc_ref[sl, :], kc_ref[sl, :], vc_ref[sl, :])
        for i in range(n_ctx_chunks):
            sl = slice(i * c, (i + 1) * c)
            o_ref[sl, :] = fwd_chunk(qc_ref[sl, :], kc_ref[sl, :], vc_ref[sl, :], g_ref[sl, :], ob_ref[sl, :])

        def back(step, carry):
            r = rows(n_lat_chunks - 1 - step)
            ob_ref[r, :] = bwd_chunk(ql_ref[r, :], kl_ref[r, :], vl_ref[r, :])
            return carry

        lax.fori_loop(0, n_lat_chunks, back, 0)

    @pl.when(t > 0)
    def _():
        for i in range(RET_TILE // c):
            r = rows((t - 1) * (RET_TILE // c) + i)
            sl = slice(i * c, (i + 1) * c)
            o_ref[sl, :] = fwd_chunk(ql_ref[r, :], kl_ref[r, :], vl_ref[r, :], g_ref[sl, :], ob_ref[r, :])


def _ret_scan(cfg, p, dec, gn_g):
    assert cfg.ctx == RET_TILE
    tiles = cfg.seq // RET_TILE
    ctx0 = cfg.n_lat // RET_TILE
    k_off = RET_QK // RET_DK
    v_off = 2 * RET_QK // RET_DV
    g_off = v_off + RET_V // RET_DV

    def tile_row(b, t):
        return jnp.where(t == 0, ctx0 + b, b * tiles + t - 1)

    return pl.pallas_call(
        functools.partial(_ret_body, n_lat_chunks=cfg.seq // CHUNK, n_ctx_chunks=cfg.ctx // CHUNK),
        grid=(cfg.batch, RET_HEADS, tiles + 1),
        in_specs=[
            pl.BlockSpec((None, 8, LANES), lambda b, h, t: (h, 0, 0)),
            pl.BlockSpec((1, RET_DV), lambda b, h, t: (0, h)),
            pl.BlockSpec((cfg.seq, RET_DK), lambda b, h, t: (b, h)),
            pl.BlockSpec((cfg.seq, RET_DK), lambda b, h, t: (b, k_off + h)),
            pl.BlockSpec((cfg.seq, RET_DV), lambda b, h, t: (b, v_off + h)),
            pl.BlockSpec((cfg.ctx, RET_DK), lambda b, h, t: (ctx0 + b, h)),
            pl.BlockSpec((cfg.ctx, RET_DK), lambda b, h, t: (ctx0 + b, k_off + h)),
            pl.BlockSpec((cfg.ctx, RET_DV), lambda b, h, t: (ctx0 + b, v_off + h)),
            pl.BlockSpec((RET_TILE, RET_DV), lambda b, h, t: (tile_row(b, t), g_off + h)),
        ],
        out_specs=pl.BlockSpec((RET_TILE, RET_DV), lambda b, h, t: (tile_row(b, t), h)),
        out_shape=jax.ShapeDtypeStruct((cfg.n_tok, RET_V), BF16),
        scratch_shapes=[pltpu.VMEM((RET_DK, RET_DV), F32), pltpu.VMEM((RET_DK, RET_DV), F32),
                        pltpu.VMEM((cfg.seq, RET_DV), F32)],
        compiler_params=_params("arbitrary", "arbitrary", "arbitrary"),
        name="ret_scan",
    )(dec, gn_g, p, p, p, p, p, p, p)


def _outproj_body(a_ref, w_ref, x_ref, mod_ref, g_ref, o_ref, *, n_k):
    kk = pl.program_id(1)

    @pl.when(kk == 0)
    def _():
        o_ref[...] = jnp.zeros_like(o_ref)

    o_ref[...] += _dot(a_ref[...], w_ref[...])

    @pl.when(kk == n_k - 1)
    def _():
        o_ref[...] = _gated_residual(x_ref[...], o_ref[...], g_ref[3:4, :], mod_ref[5:6, :], 1.0)


def _outproj(cfg, a, w, t, mods, g6, n_rows, tm=512, kc=1024):
    d = t.shape[1]
    tm = _tile(cfg, tm)
    n_k = a.shape[1] // kc
    row = _mod_row(cfg, tm)
    return pl.pallas_call(
        functools.partial(_outproj_body, n_k=n_k),
        grid=(n_rows // tm, n_k),
        in_specs=[
            pl.BlockSpec((tm, kc), lambda i, k: (i, k)),
            pl.BlockSpec((kc, d), lambda i, k: (k, 0)),
            pl.BlockSpec((tm, d), lambda i, k: (i, 0)),
            pl.BlockSpec((None, 9, d), lambda i, k: (row(i), 0, 0)),
            pl.BlockSpec((6, d), lambda i, k: (0, 0)),
        ],
        out_specs=pl.BlockSpec((tm, d), lambda i, k: (i, 0)),
        out_shape=jax.ShapeDtypeStruct((n_rows, d), F32),
        compiler_params=_params("arbitrary", "arbitrary"),
        name="outproj",
    )(a, w, t, mods, g6)


def _gmlp_body(vfull_ref, v_ref, u_ref, lng_ref, lnb_ref, ws_ref, bs_ref, w_ref, x_ref, mod_ref, g_ref,
               o_ref, mean_ref, rstd_ref, gated_ref, *, n_groups):
    grp = pl.program_id(1)
    tm = v_ref.shape[0]

    @pl.when(grp == 0)
    def _():
        v = vfull_ref[...].astype(F32)
        mu = jnp.mean(v, axis=-1, keepdims=True)
        vc = v - mu
        mean_ref[...] = mu
        rstd_ref[...] = lax.rsqrt(jnp.mean(vc * vc, axis=-1, keepdims=True) + EPS)
        o_ref[...] = jnp.zeros_like(o_ref)

    vn = ((v_ref[...].astype(F32) - mean_ref[...]) * rstd_ref[...] * lng_ref[...] + lnb_ref[...]).astype(BF16)
    ws = ws_ref[...]
    bias = bs_ref[:, 0:1]
    for i in range(tm // CHUNK):
        sl = slice(i * CHUNK, (i + 1) * CHUNK)
        mixed = _dot(ws, vn[sl, :]) + bias
        gated_ref[sl, :] = (u_ref[sl, :].astype(F32) * mixed).astype(BF16)
    o_ref[...] += _dot(gated_ref[...], w_ref[...])

    @pl.when(grp == n_groups - 1)
    def _():
        o_ref[...] = _gated_residual(x_ref[...], o_ref[...], g_ref[3:4, :], mod_ref[5:6, :], 1.0)


def _gmlp_mix(cfg, z, ln_g, ln_b, w_s, b_s, w_out, t, mods, g6, tm=512):
    n_tok, d = t.shape
    tm = _tile(cfg, tm)
    ge = GMLP_GE
    row = _mod_row(cfg, tm)
    return pl.pallas_call(
        functools.partial(_gmlp_body, n_groups=GMLP_GROUPS),
        grid=(n_tok // tm, GMLP_GROUPS),
        in_specs=[
            pl.BlockSpec((tm, GMLP_E), lambda i, g: (i, 1)),
            pl.BlockSpec((tm, ge), lambda i, g: (i, GMLP_GROUPS + g)),
            pl.BlockSpec((tm, ge), lambda i, g: (i, g)),
            pl.BlockSpec((1, ge), lambda i, g: (0, g)),
            pl.BlockSpec((1, ge), lambda i, g: (0, g)),
            pl.BlockSpec((None, CHUNK, CHUNK), lambda i, g: (g, 0, 0)),
            pl.BlockSpec((None, CHUNK, LANES), lambda i, g: (g, 0, 0)),
            pl.BlockSpec((ge, d), lambda i, g: (g, 0)),
            pl.BlockSpec((tm, d), lambda i, g: (i, 0)),
            pl.BlockSpec((None, 9, d), lambda i, g: (row(i), 0, 0)),
            pl.BlockSpec((6, d), lambda i, g: (0, 0)),
        ],
        out_specs=pl.BlockSpec((tm, d), lambda i, g: (i, 0)),
        out_shape=jax.ShapeDtypeStruct((n_tok, d), F32),
        scratch_shapes=[pltpu.VMEM((tm, 1), F32), pltpu.VMEM((tm, 1), F32), pltpu.VMEM((tm, ge), BF16)],
        compiler_params=_params("arbitrary", "arbitrary"),
        name="gmlp_mix",
    )(z, z, z, ln_g, ln_b, w_s, b_s, w_out, t, mods, g6)


CONV_TM = 256


def _conv_body(prev_ref, cur_ref, next_ref, wdw_ref, bdw_ref, lng_ref, lnb_ref, w_ref, x_ref, mod_ref, g_ref,
               o_ref, zs_ref, cz_ref, h_ref, *, lat_tiles, per_seq):
    i = pl.program_id(0)
    tm = cur_ref.shape[0]
    n_strips = cur_ref.shape[1] // LANES
    hal = CONV_HALO
    pad = CONV_K // 2
    is_ctx = i >= lat_tiles
    first = jnp.logical_or(is_ctx, i % per_seq == 0)
    last = jnp.logical_or(is_ctx, i % per_seq == per_seq - 1)
    pscale = jnp.where(first, 0.0, 1.0)
    nscale = jnp.where(last, 0.0, 1.0)
    for s in range(n_strips):
        ls = slice(s * LANES, (s + 1) * LANES)
        zs_ref[s, 0:hal, :] = prev_ref[:, ls].astype(F32) * pscale
        zs_ref[s, hal:hal + tm, :] = cur_ref[:, ls].astype(F32)
        zs_ref[s, hal + tm:hal + tm + hal, :] = next_ref[:, ls].astype(F32) * nscale

    def strip(s, carry):
        acc = jnp.zeros((tm, LANES), F32) + bdw_ref[s]
        for k in range(CONV_K):
            acc = acc + wdw_ref[s, k:k + 1, :] * zs_ref[s, hal - pad + k:hal - pad + k + tm, :]
        cz_ref[s] = acc
        return carry

    lax.fori_loop(0, n_strips, strip, 0)

    tot = jnp.zeros((tm, 1), F32)
    for s in range(n_strips):
        tot = tot + jnp.sum(cz_ref[s], axis=-1, keepdims=True)
    mu = tot / (n_strips * LANES)
    sq = jnp.zeros((tm, 1), F32)
    for s in range(n_strips):
        dlt = cz_ref[s] - mu
        sq = sq + jnp.sum(dlt * dlt, axis=-1, keepdims=True)
    rstd = lax.rsqrt(sq / (n_strips * LANES) + EPS)
    for s in range(n_strips):
        ls = slice(s * LANES, (s + 1) * LANES)
        y = (cz_ref[s] - mu) * rstd * lng_ref[:, ls] + lnb_ref[:, ls]
        h_ref[:, ls] = _silu(y).astype(BF16)
    y = _dot(h_ref[...], w_ref[...])
    o_ref[...] = _gated_residual(x_ref[...], y, g_ref[3:4, :], mod_ref[5:6, :], 1.0)


def _conv_mix(cfg, z, w_dw, b_dw, ln_g, ln_b, w_pw2, t, mods, g6):
    n_tok, d = t.shape
    tm = CONV_TM
    assert cfg.ctx == tm
    hal = CONV_HALO
    n_strips = d // LANES
    hb = tm // hal
    n_hal = n_tok // hal
    row = _mod_row(cfg, tm)
    return pl.pallas_call(
        functools.partial(_conv_body, lat_tiles=cfg.n_lat // tm, per_seq=cfg.seq // tm),
        grid=(n_tok // tm,),
        in_specs=[
            pl.BlockSpec((hal, d), lambda i: (jnp.maximum(i * hb - 1, 0), 0)),
            pl.BlockSpec((tm, d), lambda i: (i, 0)),
            pl.BlockSpec((hal, d), lambda i: (jnp.minimum((i + 1) * hb, n_hal - 1), 0)),
            pl.BlockSpec((n_strips, 32, LANES), lambda i: (0, 0, 0)),
            pl.BlockSpec((n_strips, 1, LANES), lambda i: (0, 0, 0)),
            pl.BlockSpec((1, d), lambda i: (0, 0)),
            pl.BlockSpec((1, d), lambda i: (0, 0)),
            pl.BlockSpec((d, d), lambda i: (0, 0)),
            pl.BlockSpec((tm, d), lambda i: (i, 0)),
            pl.BlockSpec((None, 9, d), lambda i: (row(i), 0, 0)),
            pl.BlockSpec((6, d), lambda i: (0, 0)),
        ],
        out_specs=pl.BlockSpec((tm, d), lambda i: (i, 0)),
        out_shape=jax.ShapeDtypeStruct((n_tok, d), F32),
        scratch_shapes=[pltpu.VMEM((n_strips, tm + 2 * hal, LANES), F32),
                        pltpu.VMEM((n_strips, tm, LANES), F32),
                        pltpu.VMEM((tm, d), BF16)],
        compiler_params=_params("arbitrary"),
        name="conv_mix",
    )(z, z, z, w_dw, b_dw, ln_g, ln_b, w_pw2, t, mods, g6)


def _rope_tables(cfg, n_ident):
    quarter = RET_DK // 4
    pos = jnp.arange(cfg.seq, dtype=jnp.int32)
    rows = (pos // GRID_W).astype(F32)
    cols = (pos % GRID_W).astype(F32)
    inv = ROPE_BASE ** (-jnp.arange(quarter, dtype=F32) / quarter)
    ar = rows[:, None] * inv[None, :]
    ac = cols[:, None] * inv[None, :]
    cos = jnp.concatenate([jnp.cos(ar), jnp.cos(ar), jnp.cos(ac), jnp.cos(ac)], axis=1)
    sin = jnp.concatenate([-jnp.sin(ar), jnp.sin(ar), -jnp.sin(ac), jnp.sin(ac)], axis=1)
    cos = jnp.concatenate([cos, jnp.ones((n_ident, RET_DK), F32)], axis=0)
    sin = jnp.concatenate([sin, jnp.zeros((n_ident, RET_DK), F32)], axis=0)
    return cos, sin


def _forward(cfg, x, c, ctx, c_ctx, ada_w, ada_b, norm_g, ffn_w_in, ffn_w_out, ret_w_in, ret_w_out,
             ret_decay_logit, ret_gn_g, gmlp_w_in, gmlp_ln_g, gmlp_ln_b, gmlp_w_s, gmlp_b_s, gmlp_w_out,
             conv_w_pw1, conv_w_dw, conv_b_dw, conv_ln_g, conv_ln_b, conv_w_pw2):
    d = x.shape[-1]
    depth = ada_w.shape[0]
    t = jnp.concatenate([x.reshape(cfg.n_lat, d), ctx.reshape(cfg.n_ctx, d)], axis=0)
    cc = jnp.zeros((ADA_ROWS, d), F32).at[:cfg.batch].set(c).at[cfg.batch].set(c_ctx)
    mods_all = _ada_table(cc, ada_w, ada_b)
    w_in_bf = ffn_w_in.astype(BF16)
    w_out_bf = ffn_w_out.astype(BF16)
    rope = _rope_tables(cfg, cfg.n_ctx)

    for i in range(depth):
        kind = i % N_MIXERS
        inst = i // N_MIXERS
        last = i == depth - 1
        mods = mods_all[i]
        g6 = norm_g[i].reshape(6, d)
        n_after = cfg.n_lat if last else cfg.n_tok
        t = _ffn_sublayer(cfg, t, mods, g6, w_in_bf, w_out_bf, i, 0, 0, cfg.n_tok)
        if kind == 0:
            p = _inproj(cfg, "ret", t, mods, g6, ret_w_in[inst].astype(BF16), rope=rope)
            log_g = jax.nn.log_sigmoid(ret_decay_logit[inst].astype(F32))
            dec = jnp.zeros((RET_HEADS, 8, LANES), F32).at[:, 0:2, :].set(
                jnp.broadcast_to(log_g.T[:, :, None], (RET_HEADS, 2, LANES)))
            a = _ret_scan(cfg, p, dec, ret_gn_g[inst].reshape(1, RET_V))
            t = _outproj(cfg, a, ret_w_out[inst].astype(BF16), t, mods, g6, n_after)
        elif kind == 1:
            z = _inproj(cfg, "gelu", t, mods, g6, gmlp_w_in[inst].astype(BF16))
            b_s = jnp.broadcast_to(gmlp_b_s[inst][:, :, None], (GMLP_GROUPS, CHUNK, LANES))
            t = _gmlp_mix(cfg, z, gmlp_ln_g[inst].reshape(1, GMLP_E), gmlp_ln_b[inst].reshape(1, GMLP_E),
                          gmlp_w_s[inst].astype(BF16), b_s, gmlp_w_out[inst].astype(BF16), t, mods, g6)
        else:
            z = _inproj(cfg, "glu", t, mods, g6, conv_w_pw1[inst].astype(BF16))
            n_strips = d // LANES
            w_dw = jnp.zeros((32, d), F32).at[:CONV_K].set(conv_w_dw[inst])
            w_dw = w_dw.reshape(32, n_strips, LANES).transpose(1, 0, 2)
            t = _conv_mix(cfg, z, w_dw, conv_b_dw[inst].reshape(n_strips, 1, LANES),
                          conv_ln_g[inst].reshape(1, d), conv_ln_b[inst].reshape(1, d),
                          conv_w_pw2[inst].astype(BF16), t, mods, g6)
        t = _ffn_sublayer(cfg, t, mods, g6, w_in_bf, w_out_bf, i, 1, 2, n_after)
    return t[:cfg.n_lat].reshape(x.shape)


def kernel(x, c, ctx, c_ctx, ada_w, ada_b, norm_g, ffn_w_in, ffn_w_out, ret_w_in, ret_w_out, ret_decay_logit,
           ret_gn_g, gmlp_w_in, gmlp_ln_g, gmlp_ln_b, gmlp_w_s, gmlp_b_s, gmlp_w_out, conv_w_pw1, conv_w_dw,
           conv_b_dw, conv_ln_g, conv_ln_b, conv_w_pw2):
    cfg = _Cfg(batch=x.shape[0], seq=x.shape[1], ctx=ctx.shape[1])
    return _forward(cfg, x, c, ctx, c_ctx, ada_w, ada_b, norm_g, ffn_w_in, ffn_w_out, ret_w_in, ret_w_out,
                    ret_decay_logit, ret_gn_g, gmlp_w_in, gmlp_ln_g, gmlp_ln_b, gmlp_w_s, gmlp_b_s, gmlp_w_out,
                    conv_w_pw1, conv_w_dw, conv_b_dw, conv_ln_g, conv_ln_b, conv_w_pw2)
```

```python
import functools
from typing import NamedTuple

import jax
import jax.numpy as jnp
from jax import lax
from jax.experimental import pallas as pl
from jax.experimental.pallas import tpu as pltpu

F32 = jnp.float32
BF16 = jnp.bfloat16

D_MODEL = 2048
DEPTH = 4
GRID_W = 64
N_MIXERS = 3
D_FF = 5632
RET_HEADS = 8
RET_DK = D_MODEL // RET_HEADS
RET_DV = 2 * RET_DK
RET_QK = RET_HEADS * RET_DK
RET_V = RET_HEADS * RET_DV
RET_IN = 2 * RET_QK + 2 * RET_V
CHUNK = 128
ROPE_BASE = 10000.0
GMLP_GROUPS = 8
GMLP_E = 3 * D_MODEL
GMLP_GE = GMLP_E // GMLP_GROUPS
CONV_K = 31
CONV_HALO = 16
EPS = 1e-6
ADA_ROWS = 8
LANES = 128

V7X_VMEM_BYTES = 64 * 1024 * 1024
VMEM_LIMIT = V7X_VMEM_BYTES - 4 * 1024 * 1024


class _Cfg(NamedTuple):
    batch: int
    seq: int
    ctx: int

    @property
    def n_lat(self):
        return self.batch * self.seq

    @property
    def n_ctx(self):
        return self.batch * self.ctx

    @property
    def n_tok(self):
        return self.n_lat + self.n_ctx


def _params(*sem):
    return pltpu.CompilerParams(dimension_semantics=sem, vmem_limit_bytes=VMEM_LIMIT)


def _tile(cfg, want):
    tm = want
    while cfg.seq % tm or cfg.n_ctx % tm:
        tm //= 2
    return tm


def _mod_row(cfg, tm):
    lat_tiles = cfg.n_lat // tm
    per_seq = cfg.seq // tm
    return lambda i: jnp.where(i < lat_tiles, i // per_seq, cfg.batch)


def _rms(x):
    return x * lax.rsqrt(jnp.mean(x * x, axis=-1, keepdims=True) + EPS)


def _standardize(x):
    xc = x - jnp.mean(x, axis=-1, keepdims=True)
    return xc * lax.rsqrt(jnp.mean(xc * xc, axis=-1, keepdims=True) + EPS)


def _modulate_in(x, g_pre, shift, scale):
    return (_rms(x) * g_pre) * (1.0 + scale) + shift


def _gated_residual(x, y, g_post, gate, weight):
    return x + weight * gate * (_rms(y) * g_post)


def _silu(x):
    return x * jax.nn.sigmoid(x)


def _gelu_tanh(x):
    return 0.5 * x * (1.0 + jnp.tanh(0.7978845608028654 * (x + 0.044715 * (x * x * x))))


def _dot(a, b):
    return jnp.dot(a, b, preferred_element_type=F32)


def _ada_body(c_ref, w_ref, b_ref, o_ref):
    s = _silu(c_ref[...]).astype(BF16)
    o_ref[...] = _dot(s, w_ref[...].astype(BF16)) + b_ref[...]


def _ada_table(cc, ada_w, ada_b):
    depth, d, n = ada_w.shape
    tn = 1024
    out = pl.pallas_call(
        _ada_body,
        grid=(depth, n // tn),
        in_specs=[
            pl.BlockSpec((ADA_ROWS, d), lambda i, j: (0, 0)),
            pl.BlockSpec((None, d, tn), lambda i, j: (i, 0, j)),
            pl.BlockSpec((None, 1, tn), lambda i, j: (i, 0, j)),
        ],
        out_specs=pl.BlockSpec((None, ADA_ROWS, tn), lambda i, j: (i, 0, j)),
        out_shape=jax.ShapeDtypeStruct((depth, ADA_ROWS, n), F32),
        compiler_params=_params("arbitrary", "arbitrary"),
        name="ada_table",
    )(cc, ada_w, ada_b.reshape(depth, 1, n))
    return out.reshape(depth, ADA_ROWS, n // d, d)


def _ffn_body(x_ref, mod_ref, g_ref, wa_ref, wb_ref, wo_ref, o_ref, h_ref, *, sub, n_ff):
    j = pl.program_id(1)

    @pl.when(j == 0)
    def _():
        h = _modulate_in(x_ref[...], g_ref[2 * sub:2 * sub + 1, :],
                         mod_ref[3 * sub:3 * sub + 1, :], mod_ref[3 * sub + 1:3 * sub + 2, :])
        h_ref[...] = h.astype(BF16)
        o_ref[...] = jnp.zeros_like(o_ref)

    h = h_ref[...]
    a = _dot(h, wa_ref[...])
    b = _dot(h, wb_ref[...])
    act = (_silu(a) * b).astype(BF16)
    o_ref[...] += _dot(act, wo_ref[...])

    @pl.when(j == n_ff - 1)
    def _():
        o_ref[...] = _gated_residual(x_ref[...], o_ref[...], g_ref[2 * sub + 1:2 * sub + 2, :],
                                     mod_ref[3 * sub + 2:3 * sub + 3, :], 0.5)


def _ffn_sublayer(cfg, t, mods, g6, w_in, w_out, layer, which, sub, n_rows, tm=512, fc=512):
    d = t.shape[1]
    tm = _tile(cfg, tm)
    f = w_out.shape[2]
    n_ff = f // fc
    row = _mod_row(cfg, tm)
    return pl.pallas_call(
        functools.partial(_ffn_body, sub=sub, n_ff=n_ff),
        grid=(n_rows // tm, n_ff),
        in_specs=[
            pl.BlockSpec((tm, d), lambda i, j: (i, 0)),
            pl.BlockSpec((None, 9, d), lambda i, j: (row(i), 0, 0)),
            pl.BlockSpec((6, d), lambda i, j: (0, 0)),
            pl.BlockSpec((None, None, d, fc), lambda i, j: (layer, which, 0, j)),
            pl.BlockSpec((None, None, d, fc), lambda i, j: (layer, which, 0, n_ff + j)),
            pl.BlockSpec((None, None, fc, d), lambda i, j: (layer, which, j, 0)),
        ],
        out_specs=pl.BlockSpec((tm, d), lambda i, j: (i, 0)),
        out_shape=jax.ShapeDtypeStruct((n_rows, d), F32),
        scratch_shapes=[pltpu.VMEM((tm, d), BF16)],
        compiler_params=_params("arbitrary", "arbitrary"),
        name="ffn_sublayer",
    )(t, mods, g6, w_in, w_in, w_out)


def _inproj_prologue(x_ref, mod_ref, g_ref, h_ref):
    @pl.when(pl.program_id(1) == 0)
    def _():
        h = _modulate_in(x_ref[...], g_ref[2:3, :], mod_ref[3:4, :], mod_ref[4:5, :])
        h_ref[...] = h.astype(BF16)


def _inproj_ret_body(x_ref, mod_ref, g_ref, cos_ref, sin_ref, w_ref, o_ref, kt_ref, h_ref, *, n_q, n_k):
    _inproj_prologue(x_ref, mod_ref, g_ref, h_ref)
    j = pl.program_id(1)
    tm, tn = o_ref.shape
    h = h_ref[...]

    def rotated_head(hd, scale):
        a = _dot(h, w_ref[:, hd * RET_DK:(hd + 1) * RET_DK])
        swapped = jnp.concatenate(
            [pltpu.roll(a[:, g * LANES:(g + 1) * LANES], LANES // 2, 1) for g in range(RET_DK // LANES)], axis=1)
        return (a * cos_ref[...] + swapped * sin_ref[...]) * scale

    @pl.when(j < n_q)
    def _():
        for hd in range(tn // RET_DK):
            o_ref[:, hd * RET_DK:(hd + 1) * RET_DK] = rotated_head(hd, 1.0).astype(BF16)

    @pl.when(jnp.logical_and(j >= n_q, j < n_q + n_k))
    def _():
        for hd in range(tn // RET_DK):
            r = rotated_head(hd, RET_DK ** -0.5)
            for ch in range(tm // CHUNK):
                kt_ref[ch, hd * RET_DK:(hd + 1) * RET_DK, :] = r[ch * CHUNK:(ch + 1) * CHUNK, :].T.astype(BF16)

    @pl.when(j >= n_q + n_k)
    def _():
        for hd in range(tn // RET_DK):
            cols = slice(hd * RET_DK, (hd + 1) * RET_DK)
            o_ref[:, cols] = _dot(h, w_ref[:, cols]).astype(BF16)


def _inproj_ret(cfg, t, mods, g6, w, rope, tm=1024, tn=1024):
    n_tok, d = t.shape
    tm = _tile(cfg, tm)
    row = _mod_row(cfg, tm)
    lat_tiles = cfg.n_lat // tm
    per_seq = cfg.seq // tm
    trow = lambda i: jnp.where(i < lat_tiles, i % per_seq, per_seq + i - lat_tiles)
    n_q = RET_QK // tn
    n_k = RET_QK // tn
    pcol = lambda j: jnp.where(j < n_q, j, jnp.maximum(j - n_k, n_q - 1))
    kcol = lambda j: jnp.clip(j - n_q, 0, n_k - 1)
    return pl.pallas_call(
        functools.partial(_inproj_ret_body, n_q=n_q, n_k=n_k),
        grid=(n_tok // tm, w.shape[1] // tn),
        in_specs=[
            pl.BlockSpec((tm, d), lambda i, j: (i, 0)),
            pl.BlockSpec((None, 9, d), lambda i, j: (row(i), 0, 0)),
            pl.BlockSpec((6, d), lambda i, j: (0, 0)),
            pl.BlockSpec((tm, RET_DK), lambda i, j: (trow(i), 0)),
            pl.BlockSpec((tm, RET_DK), lambda i, j: (trow(i), 0)),
            pl.BlockSpec((d, tn), lambda i, j: (0, j)),
        ],
        out_specs=[pl.BlockSpec((tm, tn), lambda i, j: (i, pcol(j))),
                   pl.BlockSpec((tm // CHUNK, tn, CHUNK), lambda i, j: (i, kcol(j), 0))],
        out_shape=[jax.ShapeDtypeStruct((n_tok, w.shape[1] - RET_QK), BF16),
                   jax.ShapeDtypeStruct((n_tok // CHUNK, RET_QK, CHUNK), BF16)],
        scratch_shapes=[pltpu.VMEM((tm, d), BF16)],
        compiler_params=_params("arbitrary", "arbitrary"),
        name="inproj_ret",
    )(t, mods, g6, rope[0], rope[1], w)


INPROJ_SPLIT = 2


def _inproj_gelu_body(x_ref, mod_ref, g_ref, w_ref, o_ref, h_ref):
    _inproj_prologue(x_ref, mod_ref, g_ref, h_ref)
    h = h_ref[...]
    piece = o_ref.shape[1] // INPROJ_SPLIT
    for s in range(INPROJ_SPLIT):
        cols = slice(s * piece, (s + 1) * piece)
        o_ref[:, cols] = _gelu_tanh(_dot(h, w_ref[:, cols])).astype(BF16)


def _inproj_glu_body(x_ref, mod_ref, g_ref, wa_ref, wb_ref, o_ref, h_ref):
    _inproj_prologue(x_ref, mod_ref, g_ref, h_ref)
    h = h_ref[...]
    piece = o_ref.shape[1] // INPROJ_SPLIT
    for s in range(INPROJ_SPLIT):
        cols = slice(s * piece, (s + 1) * piece)
        o_ref[:, cols] = (_dot(h, wa_ref[:, cols]) * jax.nn.sigmoid(_dot(h, wb_ref[:, cols]))).astype(BF16)


def _inproj(cfg, kind, t, mods, g6, w, tm=1024, tn=1024):
    n_tok, d = t.shape
    tm = _tile(cfg, tm)
    row = _mod_row(cfg, tm)
    common = [
        pl.BlockSpec((tm, d), lambda i, j: (i, 0)),
        pl.BlockSpec((None, 9, d), lambda i, j: (row(i), 0, 0)),
        pl.BlockSpec((6, d), lambda i, j: (0, 0)),
    ]
    if kind == "gelu":
        n_out = w.shape[1]
        body = _inproj_gelu_body
        specs = common + [pl.BlockSpec((d, tn), lambda i, j: (0, j))]
        args = (t, mods, g6, w)
    else:
        n_out = w.shape[1] // 2
        nb = n_out // tn
        body = _inproj_glu_body
        specs = common + [pl.BlockSpec((d, tn), lambda i, j: (0, j)),
                          pl.BlockSpec((d, tn), lambda i, j: (0, nb + j))]
        args = (t, mods, g6, w, w)
    return pl.pallas_call(
        body,
        grid=(n_tok // tm, n_out // tn),
        in_specs=specs,
        out_specs=pl.BlockSpec((tm, tn), lambda i, j: (i, j)),
        out_shape=jax.ShapeDtypeStruct((n_tok, n_out), BF16),
        scratch_shapes=[pltpu.VMEM((tm, d), BF16)],
        compiler_params=_params("arbitrary", "arbitrary"),
        name="inproj_" + kind,
    )(*args)


SCAN_CHUNK = 2 * CHUNK


def _ret_body(dec_ref, gn_ref, ql_ref, ktl_ref, vl_ref, gl_ref, qc_ref, ktc_ref, vc_ref, gc_ref, ol_ref, oc_ref,
              sf_ref, sb_ref, af_ref, ab_ref, *, n_lat_chunks, n_ctx_chunks):
    c = SCAN_CHUNK
    sub = c // CHUNK
    lgf = dec_ref[0:1, 0:1]
    lgb = dec_ref[1:2, 0:1]
    diff = (lax.broadcasted_iota(jnp.int32, (c, c), 0) - lax.broadcasted_iota(jnp.int32, (c, c), 1)).astype(F32)
    dmat = jnp.where(diff >= 0.0, jnp.exp(jnp.maximum(diff, 0.0) * lgf), jnp.exp(jnp.maximum(-diff, 0.0) * lgb))
    pos = lax.broadcasted_iota(jnp.int32, (c, 1), 0).astype(F32)
    lane = lax.broadcasted_iota(jnp.int32, (1, c), 1).astype(F32)
    qdec_f = jnp.exp((pos + 1.0) * lgf)
    kdec_f = jnp.exp((c - 1.0 - lane) * lgf)
    cdec_f = jnp.exp(c * lgf)
    qdec_b = jnp.exp((c - pos) * lgb)
    kdec_b = jnp.exp(lane * lgb)
    cdec_b = jnp.exp(c * lgb)

    def kt_chunk(kt_ref, i):
        return jnp.concatenate([kt_ref[i * sub + s] for s in range(sub)], axis=1)

    def pair(q_ref, kt_ref, v_ref, cf, rf, cb, rb):
        q = q_ref[rf, :]
        kt = kt_chunk(kt_ref, cf)
        v = v_ref[rf, :]
        a = _dot(q, kt) * dmat
        af_ref[rf, :] = _dot(a.astype(BF16), v) + _dot(q, sf_ref[...].astype(BF16)) * qdec_f
        sf_ref[...] = sf_ref[...] * cdec_f + _dot((kt.astype(F32) * kdec_f).astype(BF16), v)
        q = q_ref[rb, :]
        kt = kt_chunk(kt_ref, cb)
        v = v_ref[rb, :]
        ab_ref[rb, :] = _dot(q, sb_ref[...].astype(BF16)) * qdec_b
        sb_ref[...] = sb_ref[...] * cdec_b + _dot((kt.astype(F32) * kdec_b).astype(BF16), v)

    def finish(g_ref, o_ref, rows):
        o = af_ref[rows, :] + ab_ref[rows, :]
        o_ref[rows, :] = (_silu(g_ref[rows, :].astype(F32)) * (_standardize(o) * gn_ref[...])).astype(BF16)

    def chunk_rows(i):
        return pl.ds(pl.multiple_of(i * c, c), c)

    sf_ref[...] = jnp.zeros_like(sf_ref)
    sb_ref[...] = jnp.zeros_like(sb_ref)
    for i in range(n_ctx_chunks):
        ib = n_ctx_chunks - 1 - i
        pair(qc_ref, ktc_ref, vc_ref, i, slice(i * c, (i + 1) * c), ib, slice(ib * c, (ib + 1) * c))
    for i in range(n_ctx_chunks):
        finish(gc_ref, oc_ref, slice(i * c, (i + 1) * c))

    def scan_step(i, carry):
        ib = n_lat_chunks - 1 - i
        pair(ql_ref, ktl_ref, vl_ref, i, chunk_rows(i), ib, chunk_rows(ib))
        return carry

    lax.fori_loop(0, n_lat_chunks, scan_step, 0)

    def finish_step(i, carry):
        finish(gl_ref, ol_ref, chunk_rows(i))
        return carry

    lax.fori_loop(0, n_lat_chunks, finish_step, 0)


def _ret_scan(cfg, p, kt, dec, gn_g):
    assert cfg.seq % SCAN_CHUNK == 0 and cfg.ctx % SCAN_CHUNK == 0
    v_off = RET_QK // RET_DV
    g_off = v_off + RET_V // RET_DV
    ctx0 = cfg.n_lat // cfg.ctx
    return pl.pallas_call(
        functools.partial(_ret_body, n_lat_chunks=cfg.seq // SCAN_CHUNK, n_ctx_chunks=cfg.ctx // SCAN_CHUNK),
        grid=(cfg.batch, RET_HEADS),
        in_specs=[
            pl.BlockSpec((None, 8, LANES), lambda b, h: (h, 0, 0)),
            pl.BlockSpec((1, RET_DV), lambda b, h: (0, h)),
            pl.BlockSpec((cfg.seq, RET_DK), lambda b, h: (b, h)),
            pl.BlockSpec((cfg.seq // CHUNK, RET_DK, CHUNK), lambda b, h: (b, h, 0)),
            pl.BlockSpec((cfg.seq, RET_DV), lambda b, h: (b, v_off + h)),
            pl.BlockSpec((cfg.seq, RET_DV), lambda b, h: (b, g_off + h)),
            pl.BlockSpec((cfg.ctx, RET_DK), lambda b, h: (ctx0 + b, h)),
            pl.BlockSpec((cfg.ctx // CHUNK, RET_DK, CHUNK), lambda b, h: (ctx0 + b, h, 0)),
            pl.BlockSpec((cfg.ctx, RET_DV), lambda b, h: (ctx0 + b, v_off + h)),
            pl.BlockSpec((cfg.ctx, RET_DV), lambda b, h: (ctx0 + b, g_off + h)),
        ],
        out_specs=[pl.BlockSpec((cfg.seq, RET_DV), lambda b, h: (b, h)),
                   pl.BlockSpec((cfg.ctx, RET_DV), lambda b, h: (b, h))],
        out_shape=[jax.ShapeDtypeStruct((cfg.n_lat, RET_V), BF16),
                   jax.ShapeDtypeStruct((cfg.n_ctx, RET_V), BF16)],
        scratch_shapes=[pltpu.VMEM((RET_DK, RET_DV), F32), pltpu.VMEM((RET_DK, RET_DV), F32),
                        pltpu.VMEM((cfg.seq, RET_DV), F32), pltpu.VMEM((cfg.seq, RET_DV), F32)],
        compiler_params=_params("arbitrary", "arbitrary"),
        name="ret_scan",
    )(dec, gn_g, p, kt, p, p, p, kt, p, p)


def _outproj_body(al_ref, ac_ref, w_ref, x_ref, mod_ref, g_ref, o_ref, *, n_k, lat_tiles):
    i = pl.program_id(0)
    kk = pl.program_id(1)

    @pl.when(kk == 0)
    def _():
        o_ref[...] = jnp.zeros_like(o_ref)

    @pl.when(i < lat_tiles)
    def _():
        o_ref[...] += _dot(al_ref[...], w_ref[...])

    @pl.when(i >= lat_tiles)
    def _():
        o_ref[...] += _dot(ac_ref[...], w_ref[...])

    @pl.when(kk == n_k - 1)
    def _():
        o_ref[...] = _gated_residual(x_ref[...], o_ref[...], g_ref[3:4, :], mod_ref[5:6, :], 1.0)


def _outproj(cfg, a_lat, a_ctx, w, t, mods, g6, n_rows, tm=1024, kc=1024):
    d = t.shape[1]
    tm = _tile(cfg, tm)
    n_k = a_lat.shape[1] // kc
    row = _mod_row(cfg, tm)
    lat_tiles = cfg.n_lat // tm
    lat_idx = lambda i, k: (jnp.minimum(i, lat_tiles - 1), jnp.where(i < lat_tiles, k, n_k - 1))
    ctx_idx = lambda i, k: (jnp.maximum(i - lat_tiles, 0), jnp.where(i < lat_tiles, 0, k))
    return pl.pallas_call(
        functools.partial(_outproj_body, n_k=n_k, lat_tiles=lat_tiles),
        grid=(n_rows // tm, n_k),
        in_specs=[
            pl.BlockSpec((tm, kc), lat_idx),
            pl.BlockSpec((tm, kc), ctx_idx),
            pl.BlockSpec((kc, d), lambda i, k: (k, 0)),
            pl.BlockSpec((tm, d), lambda i, k: (i, 0)),
            pl.BlockSpec((None, 9, d), lambda i, k: (row(i), 0, 0)),
            pl.BlockSpec((6, d), lambda i, k: (0, 0)),
        ],
        out_specs=pl.BlockSpec((tm, d), lambda i, k: (i, 0)),
        out_shape=jax.ShapeDtypeStruct((n_rows, d), F32),
        compiler_params=_params("arbitrary", "arbitrary"),
        name="outproj",
    )(a_lat, a_ctx, w, t, mods, g6)


def _gmlp_body(vfull_ref, v_ref, u_ref, lng_ref, lnb_ref, ws_ref, bs_ref, w_ref, x_ref, mod_ref, g_ref,
               o_ref, mean_ref, rstd_ref, gated_ref, *, n_groups):
    grp = pl.program_id(1)
    tm = v_ref.shape[0]

    @pl.when(grp == 0)
    def _():
        v = vfull_ref[...].astype(F32)
        mu = jnp.mean(v, axis=-1, keepdims=True)
        vc = v - mu
        mean_ref[...] = mu
        rstd_ref[...] = lax.rsqrt(jnp.mean(vc * vc, axis=-1, keepdims=True) + EPS)
        o_ref[...] = jnp.zeros_like(o_ref)

    vn = ((v_ref[...].astype(F32) - mean_ref[...]) * rstd_ref[...] * lng_ref[...] + lnb_ref[...]).astype(BF16)
    ws = ws_ref[...]
    bias = bs_ref[:, 0:1]
    for i in range(tm // CHUNK):
        sl = slice(i * CHUNK, (i + 1) * CHUNK)
        mixed = _dot(ws, vn[sl, :]) + bias
        gated_ref[sl, :] = (u_ref[sl, :].astype(F32) * mixed).astype(BF16)
    o_ref[...] += _dot(gated_ref[...], w_ref[...])

    @pl.when(grp == n_groups - 1)
    def _():
        o_ref[...] = _gated_residual(x_ref[...], o_ref[...], g_ref[3:4, :], mod_ref[5:6, :], 1.0)


def _gmlp_mix(cfg, z, ln_g, ln_b, w_s, b_s, w_out, t, mods, g6, tm=512):
    n_tok, d = t.shape
    tm = _tile(cfg, tm)
    ge = GMLP_GE
    row = _mod_row(cfg, tm)
    return pl.pallas_call(
        functools.partial(_gmlp_body, n_groups=GMLP_GROUPS),
        grid=(n_tok // tm, GMLP_GROUPS),
        in_specs=[
            pl.BlockSpec((tm, GMLP_E), lambda i, g: (i, 1)),
            pl.BlockSpec((tm, ge), lambda i, g: (i, GMLP_GROUPS + g)),
            pl.BlockSpec((tm, ge), lambda i, g: (i, g)),
            pl.BlockSpec((1, ge), lambda i, g: (0, g)),
            pl.BlockSpec((1, ge), lambda i, g: (0, g)),
            pl.BlockSpec((None, CHUNK, CHUNK), lambda i, g: (g, 0, 0)),
            pl.BlockSpec((None, CHUNK, LANES), lambda i, g: (g, 0, 0)),
            pl.BlockSpec((ge, d), lambda i, g: (g, 0)),
            pl.BlockSpec((tm, d), lambda i, g: (i, 0)),
            pl.BlockSpec((None, 9, d), lambda i, g: (row(i), 0, 0)),
            pl.BlockSpec((6, d), lambda i, g: (0, 0)),
        ],
        out_specs=pl.BlockSpec((tm, d), lambda i, g: (i, 0)),
        out_shape=jax.ShapeDtypeStruct((n_tok, d), F32),
        scratch_shapes=[pltpu.VMEM((tm, 1), F32), pltpu.VMEM((tm, 1), F32), pltpu.VMEM((tm, ge), BF16)],
        compiler_params=_params("arbitrary", "arbitrary"),
        name="gmlp_mix",
    )(z, z, z, ln_g, ln_b, w_s, b_s, w_out, t, mods, g6)


CONV_TM = 256


def _conv_body(prev_ref, cur_ref, next_ref, wdw_ref, bdw_ref, lng_ref, lnb_ref, w_ref, x_ref, mod_ref, g_ref,
               o_ref, zs_ref, cz_ref, h_ref, *, lat_tiles, per_seq):
    i = pl.program_id(0)
    tm = cur_ref.shape[0]
    n_strips = cur_ref.shape[1] // LANES
    hal = CONV_HALO
    pad = CONV_K // 2
    is_ctx = i >= lat_tiles
    first = jnp.logical_or(is_ctx, i % per_seq == 0)
    last = jnp.logical_or(is_ctx, i % per_seq == per_seq - 1)
    pscale = jnp.where(first, 0.0, 1.0)
    nscale = jnp.where(last, 0.0, 1.0)
    for s in range(n_strips):
        ls = slice(s * LANES, (s + 1) * LANES)
        zs_ref[s, 0:hal, :] = prev_ref[:, ls].astype(F32) * pscale
        zs_ref[s, hal:hal + tm, :] = cur_ref[:, ls].astype(F32)
        zs_ref[s, hal + tm:hal + tm + hal, :] = next_ref[:, ls].astype(F32) * nscale

    def strip(s, carry):
        acc = jnp.zeros((tm, LANES), F32) + bdw_ref[s]
        for k in range(CONV_K):
            acc = acc + wdw_ref[s, k:k + 1, :] * zs_ref[s, hal - pad + k:hal - pad + k + tm, :]
        cz_ref[s] = acc
        return carry

    lax.fori_loop(0, n_strips, strip, 0)

    tot = jnp.zeros((tm, 1), F32)
    for s in range(n_strips):
        tot = tot + jnp.sum(cz_ref[s], axis=-1, keepdims=True)
    mu = tot / (n_strips * LANES)
    sq = jnp.zeros((tm, 1), F32)
    for s in range(n_strips):
        dlt = cz_ref[s] - mu
        sq = sq + jnp.sum(dlt * dlt, axis=-1, keepdims=True)
    rstd = lax.rsqrt(sq / (n_strips * LANES) + EPS)
    for s in range(n_strips):
        ls = slice(s * LANES, (s + 1) * LANES)
        y = (cz_ref[s] - mu) * rstd * lng_ref[:, ls] + lnb_ref[:, ls]
        h_ref[:, ls] = _silu(y).astype(BF16)
    y = _dot(h_ref[...], w_ref[...])
    o_ref[...] = _gated_residual(x_ref[...], y, g_ref[3:4, :], mod_ref[5:6, :], 1.0)


def _conv_mix(cfg, z, w_dw, b_dw, ln_g, ln_b, w_pw2, t, mods, g6):
    n_tok, d = t.shape
    tm = CONV_TM
    assert cfg.ctx == tm
    hal = CONV_HALO
    n_strips = d // LANES
    hb = tm // hal
    n_hal = n_tok // hal
    row = _mod_row(cfg, tm)
    return pl.pallas_call(
        functools.partial(_conv_body, lat_tiles=cfg.n_lat // tm, per_seq=cfg.seq // tm),
        grid=(n_tok // tm,),
        in_specs=[
            pl.BlockSpec((hal, d), lambda i: (jnp.maximum(i * hb - 1, 0), 0)),
            pl.BlockSpec((tm, d), lambda i: (i, 0)),
            pl.BlockSpec((hal, d), lambda i: (jnp.minimum((i + 1) * hb, n_hal - 1), 0)),
            pl.BlockSpec((n_strips, 32, LANES), lambda i: (0, 0, 0)),
            pl.BlockSpec((n_strips, 1, LANES), lambda i: (0, 0, 0)),
            pl.BlockSpec((1, d), lambda i: (0, 0)),
            pl.BlockSpec((1, d), lambda i: (0, 0)),
            pl.BlockSpec((d, d), lambda i: (0, 0)),
            pl.BlockSpec((tm, d), lambda i: (i, 0)),
            pl.BlockSpec((None, 9, d), lambda i: (row(i), 0, 0)),
            pl.BlockSpec((6, d), lambda i: (0, 0)),
        ],
        out_specs=pl.BlockSpec((tm, d), lambda i: (i, 0)),
        out_shape=jax.ShapeDtypeStruct((n_tok, d), F32),
        scratch_shapes=[pltpu.VMEM((n_strips, tm + 2 * hal, LANES), F32),
                        pltpu.VMEM((n_strips, tm, LANES), F32),
                        pltpu.VMEM((tm, d), BF16)],
        compiler_params=_params("arbitrary"),
        name="conv_mix",
    )(z, z, z, w_dw, b_dw, ln_g, ln_b, w_pw2, t, mods, g6)


def _rope_tables(cfg, n_ident):
    quarter = RET_DK // 4
    pos = jnp.arange(cfg.seq, dtype=jnp.int32)
    rows = (pos // GRID_W).astype(F32)
    cols = (pos % GRID_W).astype(F32)
    inv = ROPE_BASE ** (-jnp.arange(quarter, dtype=F32) / quarter)
    ar = rows[:, None] * inv[None, :]
    ac = cols[:, None] * inv[None, :]
    cos = jnp.concatenate([jnp.cos(ar), jnp.cos(ar), jnp.cos(ac), jnp.cos(ac)], axis=1)
    sin = jnp.concatenate([-jnp.sin(ar), jnp.sin(ar), -jnp.sin(ac), jnp.sin(ac)], axis=1)
    cos = jnp.concatenate([cos, jnp.ones((n_ident, RET_DK), F32)], axis=0)
    sin = jnp.concatenate([sin, jnp.zeros((n_ident, RET_DK), F32)], axis=0)
    return cos, sin


def _forward(cfg, x, c, ctx, c_ctx, ada_w, ada_b, norm_g, ffn_w_in, ffn_w_out, ret_w_in, ret_w_out,
             ret_decay_logit, ret_gn_g, gmlp_w_in, gmlp_ln_g, gmlp_ln_b, gmlp_w_s, gmlp_b_s, gmlp_w_out,
             conv_w_pw1, conv_w_dw, conv_b_dw, conv_ln_g, conv_ln_b, conv_w_pw2):
    d = x.shape[-1]
    depth = ada_w.shape[0]
    t = jnp.concatenate([x.reshape(cfg.n_lat, d), ctx.reshape(cfg.n_ctx, d)], axis=0)
    cc = jnp.zeros((ADA_ROWS, d), F32).at[:cfg.batch].set(c).at[cfg.batch].set(c_ctx)
    mods_all = _ada_table(cc, ada_w, ada_b)
    w_in_bf = ffn_w_in.astype(BF16)
    w_out_bf = ffn_w_out.astype(BF16)
    rope = _rope_tables(cfg, cfg.n_ctx)

    for i in range(depth):
        kind = i % N_MIXERS
        inst = i // N_MIXERS
        last = i == depth - 1
        mods = mods_all[i]
        g6 = norm_g[i].reshape(6, d)
        n_after = cfg.n_lat if last else cfg.n_tok
        t = _ffn_sublayer(cfg, t, mods, g6, w_in_bf, w_out_bf, i, 0, 0, cfg.n_tok)
        if kind == 0:
            p, kt = _inproj_ret(cfg, t, mods, g6, ret_w_in[inst].astype(BF16), rope)
            log_g = jax.nn.log_sigmoid(ret_decay_logit[inst].astype(F32))
            dec = jnp.zeros((RET_HEADS, 8, LANES), F32).at[:, 0:2, :].set(
                jnp.broadcast_to(log_g.T[:, :, None], (RET_HEADS, 2, LANES)))
            a_lat, a_ctx = _ret_scan(cfg, p, kt, dec, ret_gn_g[inst].reshape(1, RET_V))
            t = _outproj(cfg, a_lat, a_ctx, ret_w_out[inst].astype(BF16), t, mods, g6, n_after)
        elif kind == 1:
            z = _inproj(cfg, "gelu", t, mods, g6, gmlp_w_in[inst].astype(BF16))
            b_s = jnp.broadcast_to(gmlp_b_s[inst][:, :, None], (GMLP_GROUPS, CHUNK, LANES))
            t = _gmlp_mix(cfg, z, gmlp_ln_g[inst].reshape(1, GMLP_E), gmlp_ln_b[inst].reshape(1, GMLP_E),
                          gmlp_w_s[inst].astype(BF16), b_s, gmlp_w_out[inst].astype(BF16), t, mods, g6)
        else:
            z = _inproj(cfg, "glu", t, mods, g6, conv_w_pw1[inst].astype(BF16))
            n_strips = d // LANES
            w_dw = jnp.zeros((32, d), F32).at[:CONV_K].set(conv_w_dw[inst])
            w_dw = w_dw.reshape(32, n_strips, LANES).transpose(1, 0, 2)
            t = _conv_mix(cfg, z, w_dw, conv_b_dw[inst].reshape(n_strips, 1, LANES),
                          conv_ln_g[inst].reshape(1, d), conv_ln_b[inst].reshape(1, d),
                          conv_w_pw2[inst].astype(BF16), t, mods, g6)
        t = _ffn_sublayer(cfg, t, mods, g6, w_in_bf, w_out_bf, i, 1, 2, n_after)
    return t[:cfg.n_lat].reshape(x.shape)


def kernel(x, c, ctx, c_ctx, ada_w, ada_b, norm_g, ffn_w_in, ffn_w_out, ret_w_in, ret_w_out, ret_decay_logit,
           ret_gn_g, gmlp_w_in, gmlp_ln_g, gmlp_ln_b, gmlp_w_s, gmlp_b_s, gmlp_w_out, conv_w_pw1, conv_w_dw,
           conv_b_dw, conv_ln_g, conv_ln_b, conv_w_pw2):
    cfg = _Cfg(batch=x.shape[0], seq=x.shape[1], ctx=ctx.shape[1])
    return _forward(cfg, x, c, ctx, c_ctx, ada_w, ada_b, norm_g, ffn_w_in, ffn_w_out, ret_w_in, ret_w_out,
                    ret_decay_logit, ret_gn_g, gmlp_w_in, gmlp_ln_g, gmlp_ln_b, gmlp_w_s, gmlp_b_s, gmlp_w_out,
                    conv_w_pw1, conv_w_dw, conv_b_dw, conv_ln_g, conv_ln_b, conv_w_pw2)
```

```python
import functools
from typing import NamedTuple

import jax
import jax.numpy as jnp
from jax import lax
from jax.experimental import pallas as pl
from jax.experimental.pallas import tpu as pltpu

F32 = jnp.float32
BF16 = jnp.bfloat16

D_MODEL = 2048
DEPTH = 4
GRID_W = 64
N_MIXERS = 3
D_FF = 5632
RET_HEADS = 8
RET_DK = D_MODEL // RET_HEADS
RET_DV = 2 * RET_DK
RET_QK = RET_HEADS * RET_DK
RET_V = RET_HEADS * RET_DV
RET_IN = 2 * RET_QK + 2 * RET_V
CHUNK = 128
ROPE_BASE = 10000.0
GMLP_GROUPS = 8
GMLP_E = 3 * D_MODEL
GMLP_GE = GMLP_E // GMLP_GROUPS
CONV_K = 31
CONV_HALO = 16
EPS = 1e-6
ADA_ROWS = 8
LANES = 128

V7X_VMEM_BYTES = 64 * 1024 * 1024
VMEM_LIMIT = V7X_VMEM_BYTES - 4 * 1024 * 1024


class _Cfg(NamedTuple):
    batch: int
    seq: int
    ctx: int

    @property
    def n_lat(self):
        return self.batch * self.seq

    @property
    def n_ctx(self):
        return self.batch * self.ctx

    @property
    def n_tok(self):
        return self.n_lat + self.n_ctx


def _params(*sem):
    return pltpu.CompilerParams(dimension_semantics=sem, vmem_limit_bytes=VMEM_LIMIT)


def _tile(cfg, want):
    tm = want
    while cfg.seq % tm or cfg.n_ctx % tm:
        tm //= 2
    return tm


def _mod_row(cfg, tm):
    lat_tiles = cfg.n_lat // tm
    per_seq = cfg.seq // tm
    return lambda i: jnp.where(i < lat_tiles, i // per_seq, cfg.batch)


NORM_ROWS = 16


def _modulate_rows(x_ref, mod_ref, g_ref, h_ref, sub):
    x = x_ref[...]
    xn = x * lax.rsqrt(jnp.mean(x * x, axis=-1, keepdims=True) + EPS)
    h = (xn * g_ref[2 * sub:2 * sub + 1, :]) * (1.0 + mod_ref[3 * sub + 1:3 * sub + 2, :])
    h_ref[...] = (h + mod_ref[3 * sub:3 * sub + 1, :]).astype(BF16)


def _gated_residual_rows(x_ref, o_ref, mod_ref, g_ref, rstd_ref, sub, weight):
    blocks = [slice(i, i + NORM_ROWS) for i in range(0, x_ref.shape[0], NORM_ROWS)]
    for rows in blocks:
        v = o_ref[rows, :]
        rstd_ref[rows, :] = lax.rsqrt(jnp.mean(v * v, axis=-1, keepdims=True) + EPS)
    for rows in blocks:
        yn = o_ref[rows, :] * rstd_ref[rows, :]
        gate = weight * mod_ref[3 * sub + 2:3 * sub + 3, :]
        o_ref[rows, :] = x_ref[rows, :] + gate * (yn * g_ref[2 * sub + 1:2 * sub + 2, :])


def _standardize(x):
    xc = x - jnp.mean(x, axis=-1, keepdims=True)
    return xc * lax.rsqrt(jnp.mean(xc * xc, axis=-1, keepdims=True) + EPS)


def _silu(x):
    return x * jax.nn.sigmoid(x)


def _gelu_tanh(x):
    return 0.5 * x * (1.0 + jnp.tanh(0.7978845608028654 * (x + 0.044715 * (x * x * x))))


def _dot(a, b):
    return jnp.dot(a, b, preferred_element_type=F32)


def _ada_body(c_ref, w_ref, b_ref, o_ref):
    s = _silu(c_ref[...]).astype(BF16)
    o_ref[...] = _dot(s, w_ref[...].astype(BF16)) + b_ref[...]


def _ada_table(cc, ada_w, ada_b):
    depth, d, n = ada_w.shape
    tn = 1024
    out = pl.pallas_call(
        _ada_body,
        grid=(depth, n // tn),
        in_specs=[
            pl.BlockSpec((ADA_ROWS, d), lambda i, j: (0, 0)),
            pl.BlockSpec((None, d, tn), lambda i, j: (i, 0, j)),
            pl.BlockSpec((None, 1, tn), lambda i, j: (i, 0, j)),
        ],
        out_specs=pl.BlockSpec((None, ADA_ROWS, tn), lambda i, j: (i, 0, j)),
        out_shape=jax.ShapeDtypeStruct((depth, ADA_ROWS, n), F32),
        compiler_params=_params("arbitrary", "arbitrary"),
        name="ada_table",
    )(cc, ada_w, ada_b.reshape(depth, 1, n))
    return out.reshape(depth, ADA_ROWS, n // d, d)


def _ffn_chunk(h_ref, o_ref, wa, wb, wo):
    h = h_ref[...]
    act = (_silu(_dot(h, wa)) * _dot(h, wb)).astype(BF16)
    o_ref[...] += _dot(act, wo)


def _ffn_first_body(x_ref, mod_ref, g_ref, wa32_ref, wb32_ref, wo32_ref, o_ref, wa_ref, wb_ref, wo_ref,
                    h_ref, rstd_ref, *, sub, n_ff):
    j = pl.program_id(0)

    @pl.when(j == 0)
    def _():
        _modulate_rows(x_ref, mod_ref, g_ref, h_ref, sub)
        o_ref[...] = jnp.zeros_like(o_ref)

    wa_ref[...] = wa32_ref[...].astype(BF16)
    wb_ref[...] = wb32_ref[...].astype(BF16)
    wo_ref[...] = wo32_ref[...].astype(BF16)
    _ffn_chunk(h_ref, o_ref, wa_ref[...], wb_ref[...], wo_ref[...])

    @pl.when(j == n_ff - 1)
    def _():
        _gated_residual_rows(x_ref, o_ref, mod_ref, g_ref, rstd_ref, sub, 0.5)


def _ffn_body(x_ref, mod_ref, g_ref, wa_ref, wb_ref, wo_ref, first_ref, o_ref, h_ref, rstd_ref, *, sub, n_ff):
    i = pl.program_id(0)
    j = pl.program_id(1)

    @pl.when(jnp.logical_and(i == 0, j == 0))
    def _():
        o_ref[...] = first_ref[...]

    @pl.when(i > 0)
    def _():
        @pl.when(j == 0)
        def _():
            _modulate_rows(x_ref, mod_ref, g_ref, h_ref, sub)
            o_ref[...] = jnp.zeros_like(o_ref)

        _ffn_chunk(h_ref, o_ref, wa_ref[...], wb_ref[...], wo_ref[...])

        @pl.when(j == n_ff - 1)
        def _():
            _gated_residual_rows(x_ref, o_ref, mod_ref, g_ref, rstd_ref, sub, 0.5)


def _ffn_sublayer(cfg, t, mods, g6, w_in, w_out, layer, which, sub, n_rows, tm=512, fc=512, fc_first=256):
    d = t.shape[1]
    tm = _tile(cfg, tm)
    f = w_out.shape[2]
    row = _mod_row(cfg, tm)
    scratch = [pltpu.VMEM((tm, d), BF16), pltpu.VMEM((tm, 1), F32)]

    n1 = f // fc_first
    first, wa, wb, wo = pl.pallas_call(
        functools.partial(_ffn_first_body, sub=sub, n_ff=n1),
        grid=(n1,),
        in_specs=[
            pl.BlockSpec((tm, d), lambda j: (0, 0)),
            pl.BlockSpec((None, 9, d), lambda j: (0, 0, 0)),
            pl.BlockSpec((6, d), lambda j: (0, 0)),
            pl.BlockSpec((None, None, d, fc_first), lambda j: (layer, which, 0, j)),
            pl.BlockSpec((None, None, d, fc_first), lambda j: (layer, which, 0, n1 + j)),
            pl.BlockSpec((None, None, fc_first, d), lambda j: (layer, which, j, 0)),
        ],
        out_specs=[pl.BlockSpec((tm, d), lambda j: (0, 0)),
                   pl.BlockSpec((d, fc_first), lambda j: (0, j)),
                   pl.BlockSpec((d, fc_first), lambda j: (0, j)),
                   pl.BlockSpec((fc_first, d), lambda j: (j, 0))],
        out_shape=[jax.ShapeDtypeStruct((tm, d), F32), jax.ShapeDtypeStruct((d, f), BF16),
                   jax.ShapeDtypeStruct((d, f), BF16), jax.ShapeDtypeStruct((f, d), BF16)],
        scratch_shapes=scratch,
        compiler_params=_params("arbitrary"),
        name="ffn_first_tile",
    )(t, mods, g6, w_in, w_in, w_out)

    n_ff = f // fc
    chunk = lambda i, j: jnp.where(i == 0, 0, j)
    return pl.pallas_call(
        functools.partial(_ffn_body, sub=sub, n_ff=n_ff),
        grid=(n_rows // tm, n_ff),
        in_specs=[
            pl.BlockSpec((tm, d), lambda i, j: (i, 0)),
            pl.BlockSpec((None, 9, d), lambda i, j: (row(i), 0, 0)),
            pl.BlockSpec((6, d), lambda i, j: (0, 0)),
            pl.BlockSpec((d, fc), lambda i, j: (0, chunk(i, j))),
            pl.BlockSpec((d, fc), lambda i, j: (0, chunk(i, j))),
            pl.BlockSpec((fc, d), lambda i, j: (chunk(i, j), 0)),
            pl.BlockSpec((tm, d), lambda i, j: (0, 0)),
        ],
        out_specs=pl.BlockSpec((tm, d), lambda i, j: (i, 0)),
        out_shape=jax.ShapeDtypeStruct((n_rows, d), F32),
        scratch_shapes=scratch,
        compiler_params=_params("arbitrary", "arbitrary"),
        name="ffn_sublayer",
    )(t, mods, g6, wa, wb, wo, first)


def _inproj_prologue(x_ref, mod_ref, g_ref, h_ref):
    @pl.when(pl.program_id(1) == 0)
    def _():
        _modulate_rows(x_ref, mod_ref, g_ref, h_ref, 1)


def _inproj_ret_body(x_ref, mod_ref, g_ref, cos_ref, sin_ref, w_ref, o_ref, kt_ref, h_ref, *, n_q, n_k):
    _inproj_prologue(x_ref, mod_ref, g_ref, h_ref)
    j = pl.program_id(1)
    tm, tn = o_ref.shape
    h = h_ref[...]

    def rotated_head(hd, scale):
        a = _dot(h, w_ref[:, hd * RET_DK:(hd + 1) * RET_DK])
        swapped = jnp.concatenate(
            [pltpu.roll(a[:, g * LANES:(g + 1) * LANES], LANES // 2, 1) for g in range(RET_DK // LANES)], axis=1)
        return (a * cos_ref[...] + swapped * sin_ref[...]) * scale

    @pl.when(j < n_q)
    def _():
        for hd in range(tn // RET_DK):
            o_ref[:, hd * RET_DK:(hd + 1) * RET_DK] = rotated_head(hd, 1.0).astype(BF16)

    @pl.when(jnp.logical_and(j >= n_q, j < n_q + n_k))
    def _():
        for hd in range(tn // RET_DK):
            r = rotated_head(hd, RET_DK ** -0.5)
            for ch in range(tm // CHUNK):
                kt_ref[ch, hd * RET_DK:(hd + 1) * RET_DK, :] = r[ch * CHUNK:(ch + 1) * CHUNK, :].T.astype(BF16)

    @pl.when(j >= n_q + n_k)
    def _():
        for hd in range(tn // RET_DK):
            cols = slice(hd * RET_DK, (hd + 1) * RET_DK)
            o_ref[:, cols] = _dot(h, w_ref[:, cols]).astype(BF16)


def _inproj_ret(cfg, t, mods, g6, w_all, inst, rope, tm=1024, tn=1024):
    n_tok, d = t.shape
    n_proj = w_all.shape[2]
    tm = _tile(cfg, tm)
    row = _mod_row(cfg, tm)
    lat_tiles = cfg.n_lat // tm
    per_seq = cfg.seq // tm
    trow = lambda i: jnp.where(i < lat_tiles, i % per_seq, per_seq + i - lat_tiles)
    n_q = RET_QK // tn
    n_k = RET_QK // tn
    pcol = lambda j: jnp.where(j < n_q, j, jnp.maximum(j - n_k, n_q - 1))
    kcol = lambda j: jnp.clip(j - n_q, 0, n_k - 1)
    return pl.pallas_call(
        functools.partial(_inproj_ret_body, n_q=n_q, n_k=n_k),
        grid=(n_tok // tm, n_proj // tn),
        in_specs=[
            pl.BlockSpec((tm, d), lambda i, j: (i, 0)),
            pl.BlockSpec((None, 9, d), lambda i, j: (row(i), 0, 0)),
            pl.BlockSpec((6, d), lambda i, j: (0, 0)),
            pl.BlockSpec((tm, RET_DK), lambda i, j: (trow(i), 0)),
            pl.BlockSpec((tm, RET_DK), lambda i, j: (trow(i), 0)),
            pl.BlockSpec((None, d, tn), lambda i, j: (inst, 0, j)),
        ],
        out_specs=[pl.BlockSpec((tm, tn), lambda i, j: (i, pcol(j))),
                   pl.BlockSpec((tm // CHUNK, tn, CHUNK), lambda i, j: (i, kcol(j), 0))],
        out_shape=[jax.ShapeDtypeStruct((n_tok, n_proj - RET_QK), BF16),
                   jax.ShapeDtypeStruct((n_tok // CHUNK, RET_QK, CHUNK), BF16)],
        scratch_shapes=[pltpu.VMEM((tm, d), BF16)],
        compiler_params=_params("arbitrary", "arbitrary"),
        name="inproj_ret",
    )(t, mods, g6, rope[0], rope[1], w_all)


INPROJ_SPLIT = 4


def _inproj_gelu_body(x_ref, mod_ref, g_ref, w_ref, o_ref, h_ref):
    _inproj_prologue(x_ref, mod_ref, g_ref, h_ref)
    h = h_ref[...]
    piece = o_ref.shape[1] // INPROJ_SPLIT
    for s in range(INPROJ_SPLIT):
        cols = slice(s * piece, (s + 1) * piece)
        o_ref[:, cols] = _gelu_tanh(_dot(h, w_ref[:, cols])).astype(BF16)


def _inproj_glu_body(x_ref, mod_ref, g_ref, wa_ref, wb_ref, o_ref, h_ref):
    _inproj_prologue(x_ref, mod_ref, g_ref, h_ref)
    h = h_ref[...]
    piece = o_ref.shape[1] // INPROJ_SPLIT
    for s in range(INPROJ_SPLIT):
        cols = slice(s * piece, (s + 1) * piece)
        o_ref[:, cols] = (_dot(h, wa_ref[:, cols]) * jax.nn.sigmoid(_dot(h, wb_ref[:, cols]))).astype(BF16)


def _inproj(cfg, kind, t, mods, g6, w, tm=1024, tn=1024):
    n_tok, d = t.shape
    tm = _tile(cfg, tm)
    row = _mod_row(cfg, tm)
    common = [
        pl.BlockSpec((tm, d), lambda i, j: (i, 0)),
        pl.BlockSpec((None, 9, d), lambda i, j: (row(i), 0, 0)),
        pl.BlockSpec((6, d), lambda i, j: (0, 0)),
    ]
    if kind == "gelu":
        n_out = w.shape[1]
        body = _inproj_gelu_body
        specs = common + [pl.BlockSpec((d, tn), lambda i, j: (0, j))]
        args = (t, mods, g6, w)
    else:
        n_out = w.shape[1] // 2
        nb = n_out // tn
        body = _inproj_glu_body
        specs = common + [pl.BlockSpec((d, tn), lambda i, j: (0, j)),
                          pl.BlockSpec((d, tn), lambda i, j: (0, nb + j))]
        args = (t, mods, g6, w, w)
    return pl.pallas_call(
        body,
        grid=(n_tok // tm, n_out // tn),
        in_specs=specs,
        out_specs=pl.BlockSpec((tm, tn), lambda i, j: (i, j)),
        out_shape=jax.ShapeDtypeStruct((n_tok, n_out), BF16),
        scratch_shapes=[pltpu.VMEM((tm, d), BF16)],
        compiler_params=_params("arbitrary", "arbitrary"),
        name="inproj_" + kind,
    )(*args)


SCAN_CHUNK = 2 * CHUNK


def _ret_body(dec_ref, gn_ref, ql_ref, ktl_ref, vl_ref, gl_ref, qc_ref, ktc_ref, vc_ref, gc_ref, ol_ref, oc_ref,
              sf_ref, sb_ref, af_ref, ab_ref, *, n_lat_chunks, n_ctx_chunks):
    c = SCAN_CHUNK
    sub = c // CHUNK
    lgf = dec_ref[0:1, 0:1]
    lgb = dec_ref[1:2, 0:1]
    diff = (lax.broadcasted_iota(jnp.int32, (c, c), 0) - lax.broadcasted_iota(jnp.int32, (c, c), 1)).astype(F32)
    dmat = jnp.where(diff >= 0.0, jnp.exp(jnp.maximum(diff, 0.0) * lgf), jnp.exp(jnp.maximum(-diff, 0.0) * lgb))
    pos = lax.broadcasted_iota(jnp.int32, (c, 1), 0).astype(F32)
    lane = lax.broadcasted_iota(jnp.int32, (1, c), 1).astype(F32)
    qdec_f = jnp.exp((pos + 1.0) * lgf)
    kdec_f = jnp.exp((c - 1.0 - lane) * lgf)
    cdec_f = jnp.exp(c * lgf)
    qdec_b = jnp.exp((c - pos) * lgb)
    kdec_b = jnp.exp(lane * lgb)
    cdec_b = jnp.exp(c * lgb)

    def kt_chunk(kt_ref, i):
        return jnp.concatenate([kt_ref[i * sub + s] for s in range(sub)], axis=1)

    def pair(q_ref, kt_ref, v_ref, cf, rf, cb, rb):
        q = q_ref[rf, :]
        kt = kt_chunk(kt_ref, cf)
        v = v_ref[rf, :]
        a = _dot(q, kt) * dmat
        af_ref[rf, :] = _dot(a.astype(BF16), v) + _dot(q, sf_ref[...].astype(BF16)) * qdec_f
        sf_ref[...] = sf_ref[...] * cdec_f + _dot((kt.astype(F32) * kdec_f).astype(BF16), v)
        q = q_ref[rb, :]
        kt = kt_chunk(kt_ref, cb)
        v = v_ref[rb, :]
        ab_ref[rb, :] = _dot(q, sb_ref[...].astype(BF16)) * qdec_b
        sb_ref[...] = sb_ref[...] * cdec_b + _dot((kt.astype(F32) * kdec_b).astype(BF16), v)

    def finish(g_ref, o_ref, rows):
        o = af_ref[rows, :] + ab_ref[rows, :]
        o_ref[rows, :] = (_silu(g_ref[rows, :].astype(F32)) * (_standardize(o) * gn_ref[...])).astype(BF16)

    def chunk_rows(i):
        return pl.ds(pl.multiple_of(i * c, c), c)

    sf_ref[...] = jnp.zeros_like(sf_ref)
    sb_ref[...] = jnp.zeros_like(sb_ref)
    for i in range(n_ctx_chunks):
        ib = n_ctx_chunks - 1 - i
        pair(qc_ref, ktc_ref, vc_ref, i, slice(i * c, (i + 1) * c), ib, slice(ib * c, (ib + 1) * c))
    for i in range(n_ctx_chunks):
        finish(gc_ref, oc_ref, slice(i * c, (i + 1) * c))

    def scan_step(i, carry):
        ib = n_lat_chunks - 1 - i
        pair(ql_ref, ktl_ref, vl_ref, i, chunk_rows(i), ib, chunk_rows(ib))
        return carry

    lax.fori_loop(0, n_lat_chunks, scan_step, 0)

    def finish_step(i, carry):
        finish(gl_ref, ol_ref, chunk_rows(i))
        return carry

    lax.fori_loop(0, n_lat_chunks, finish_step, 0)


def _ret_scan(cfg, p, kt, dec, gn_g):
    assert cfg.seq % SCAN_CHUNK == 0 and cfg.ctx % SCAN_CHUNK == 0
    v_off = RET_QK // RET_DV
    g_off = v_off + RET_V // RET_DV
    ctx0 = cfg.n_lat // cfg.ctx
    return pl.pallas_call(
        functools.partial(_ret_body, n_lat_chunks=cfg.seq // SCAN_CHUNK, n_ctx_chunks=cfg.ctx // SCAN_CHUNK),
        grid=(cfg.batch, RET_HEADS),
        in_specs=[
            pl.BlockSpec((None, 8, LANES), lambda b, h: (h, 0, 0)),
            pl.BlockSpec((1, RET_DV), lambda b, h: (0, h)),
            pl.BlockSpec((cfg.seq, RET_DK), lambda b, h: (b, h)),
            pl.BlockSpec((cfg.seq // CHUNK, RET_DK, CHUNK), lambda b, h: (b, h, 0)),
            pl.BlockSpec((cfg.seq, RET_DV), lambda b, h: (b, v_off + h)),
            pl.BlockSpec((cfg.seq, RET_DV), lambda b, h: (b, g_off + h)),
            pl.BlockSpec((cfg.ctx, RET_DK), lambda b, h: (ctx0 + b, h)),
            pl.BlockSpec((cfg.ctx // CHUNK, RET_DK, CHUNK), lambda b, h: (ctx0 + b, h, 0)),
            pl.BlockSpec((cfg.ctx, RET_DV), lambda b, h: (ctx0 + b, v_off + h)),
            pl.BlockSpec((cfg.ctx, RET_DV), lambda b, h: (ctx0 + b, g_off + h)),
        ],
        out_specs=[pl.BlockSpec((cfg.seq, RET_DV), lambda b, h: (b, h)),
                   pl.BlockSpec((cfg.ctx, RET_DV), lambda b, h: (b, h))],
        out_shape=[jax.ShapeDtypeStruct((cfg.n_lat, RET_V), BF16),
                   jax.ShapeDtypeStruct((cfg.n_ctx, RET_V), BF16)],
        scratch_shapes=[pltpu.VMEM((RET_DK, RET_DV), F32), pltpu.VMEM((RET_DK, RET_DV), F32),
                        pltpu.VMEM((cfg.seq, RET_DV), F32), pltpu.VMEM((cfg.seq, RET_DV), F32)],
        compiler_params=_params("arbitrary", "arbitrary"),
        name="ret_scan",
    )(dec, gn_g, p, kt, p, p, p, kt, p, p)


def _outproj_body(al_ref, ac_ref, w_ref, x_ref, mod_ref, g_ref, o_ref, rstd_ref, *, n_k, lat_tiles):
    i = pl.program_id(0)
    kk = pl.program_id(1)

    @pl.when(kk == 0)
    def _():
        o_ref[...] = jnp.zeros_like(o_ref)

    @pl.when(i < lat_tiles)
    def _():
        o_ref[...] += _dot(al_ref[...], w_ref[...])

    @pl.when(i >= lat_tiles)
    def _():
        o_ref[...] += _dot(ac_ref[...], w_ref[...])

    @pl.when(kk == n_k - 1)
    def _():
        _gated_residual_rows(x_ref, o_ref, mod_ref, g_ref, rstd_ref, 1, 1.0)


def _outproj(cfg, a_lat, a_ctx, w_all, inst, t, mods, g6, n_rows, tm=1024, kc=1024):
    d = t.shape[1]
    tm = _tile(cfg, tm)
    n_k = a_lat.shape[1] // kc
    row = _mod_row(cfg, tm)
    lat_tiles = cfg.n_lat // tm
    lat_idx = lambda i, k: (jnp.minimum(i, lat_tiles - 1), jnp.where(i < lat_tiles, k, n_k - 1))
    ctx_idx = lambda i, k: (jnp.maximum(i - lat_tiles, 0), jnp.where(i < lat_tiles, 0, k))
    return pl.pallas_call(
        functools.partial(_outproj_body, n_k=n_k, lat_tiles=lat_tiles),
        grid=(n_rows // tm, n_k),
        in_specs=[
            pl.BlockSpec((tm, kc), lat_idx),
            pl.BlockSpec((tm, kc), ctx_idx),
            pl.BlockSpec((None, kc, d), lambda i, k: (inst, k, 0)),
            pl.BlockSpec((tm, d), lambda i, k: (i, 0)),
            pl.BlockSpec((None, 9, d), lambda i, k: (row(i), 0, 0)),
            pl.BlockSpec((6, d), lambda i, k: (0, 0)),
        ],
        out_specs=pl.BlockSpec((tm, d), lambda i, k: (i, 0)),
        out_shape=jax.ShapeDtypeStruct((n_rows, d), F32),
        scratch_shapes=[pltpu.VMEM((tm, 1), F32)],
        compiler_params=_params("arbitrary", "arbitrary"),
        name="outproj",
    )(a_lat, a_ctx, w_all, t, mods, g6)


def _gmlp_body(vfull_ref, v_ref, u_ref, lng_ref, lnb_ref, ws_ref, bs_ref, w_ref, x_ref, mod_ref, g_ref,
               o_ref, mean_ref, rstd_ref, gated_ref, *, n_groups):
    grp = pl.program_id(1)
    tm = v_ref.shape[0]

    @pl.when(grp == 0)
    def _():
        v = vfull_ref[...].astype(F32)
        mu = jnp.mean(v, axis=-1, keepdims=True)
        vc = v - mu
        mean_ref[...] = mu
        rstd_ref[...] = lax.rsqrt(jnp.mean(vc * vc, axis=-1, keepdims=True) + EPS)
        o_ref[...] = jnp.zeros_like(o_ref)

    vn = ((v_ref[...].astype(F32) - mean_ref[...]) * rstd_ref[...] * lng_ref[...] + lnb_ref[...]).astype(BF16)
    ws = ws_ref[...]
    bias = bs_ref[:, 0:1]
    for i in range(tm // CHUNK):
        sl = slice(i * CHUNK, (i + 1) * CHUNK)
        mixed = _dot(ws, vn[sl, :]) + bias
        gated_ref[sl, :] = (u_ref[sl, :].astype(F32) * mixed).astype(BF16)
    o_ref[...] += _dot(gated_ref[...], w_ref[...])

    @pl.when(grp == n_groups - 1)
    def _():
        _gated_residual_rows(x_ref, o_ref, mod_ref, g_ref, rstd_ref, 1, 1.0)


def _gmlp_mix(cfg, z, ln_g, ln_b, w_s, b_s, w_out, t, mods, g6, tm=512):
    n_tok, d = t.shape
    tm = _tile(cfg, tm)
    ge = GMLP_GE
    row = _mod_row(cfg, tm)
    return pl.pallas_call(
        functools.partial(_gmlp_body, n_groups=GMLP_GROUPS),
        grid=(n_tok // tm, GMLP_GROUPS),
        in_specs=[
            pl.BlockSpec((tm, GMLP_E), lambda i, g: (i, 1)),
            pl.BlockSpec((tm, ge), lambda i, g: (i, GMLP_GROUPS + g)),
            pl.BlockSpec((tm, ge), lambda i, g: (i, g)),
            pl.BlockSpec((1, ge), lambda i, g: (0, g)),
            pl.BlockSpec((1, ge), lambda i, g: (0, g)),
            pl.BlockSpec((None, CHUNK, CHUNK), lambda i, g: (g, 0, 0)),
            pl.BlockSpec((None, CHUNK, LANES), lambda i, g: (g, 0, 0)),
            pl.BlockSpec((ge, d), lambda i, g: (g, 0)),
            pl.BlockSpec((tm, d), lambda i, g: (i, 0)),
            pl.BlockSpec((None, 9, d), lambda i, g: (row(i), 0, 0)),
            pl.BlockSpec((6, d), lambda i, g: (0, 0)),
        ],
        out_specs=pl.BlockSpec((tm, d), lambda i, g: (i, 0)),
        out_shape=jax.ShapeDtypeStruct((n_tok, d), F32),
        scratch_shapes=[pltpu.VMEM((tm, 1), F32), pltpu.VMEM((tm, 1), F32), pltpu.VMEM((tm, ge), BF16)],
        compiler_params=_params("arbitrary", "arbitrary"),
        name="gmlp_mix",
    )(z, z, z, ln_g, ln_b, w_s, b_s, w_out, t, mods, g6)


CONV_TM = 256


def _conv_body(prev_ref, cur_ref, next_ref, wdw_ref, bdw_ref, lng_ref, lnb_ref, w_ref, x_ref, mod_ref, g_ref,
               o_ref, zs_ref, cz_ref, h_ref, rstd_ref, *, lat_tiles, per_seq):
    i = pl.program_id(0)
    tm = cur_ref.shape[0]
    n_strips = cur_ref.shape[1] // LANES
    hal = CONV_HALO
    pad = CONV_K // 2
    is_ctx = i >= lat_tiles
    first = jnp.logical_or(is_ctx, i % per_seq == 0)
    last = jnp.logical_or(is_ctx, i % per_seq == per_seq - 1)
    pscale = jnp.where(first, 0.0, 1.0)
    nscale = jnp.where(last, 0.0, 1.0)
    for s in range(n_strips):
        ls = slice(s * LANES, (s + 1) * LANES)
        zs_ref[s, 0:hal, :] = prev_ref[:, ls].astype(F32) * pscale
        zs_ref[s, hal:hal + tm, :] = cur_ref[:, ls].astype(F32)
        zs_ref[s, hal + tm:hal + tm + hal, :] = next_ref[:, ls].astype(F32) * nscale

    def strip(s, carry):
        acc = jnp.zeros((tm, LANES), F32) + bdw_ref[s]
        for k in range(CONV_K):
            acc = acc + wdw_ref[s, k:k + 1, :] * zs_ref[s, hal - pad + k:hal - pad + k + tm, :]
        cz_ref[s] = acc
        return carry

    lax.fori_loop(0, n_strips, strip, 0)

    tot = jnp.zeros((tm, 1), F32)
    for s in range(n_strips):
        tot = tot + jnp.sum(cz_ref[s], axis=-1, keepdims=True)
    mu = tot / (n_strips * LANES)
    sq = jnp.zeros((tm, 1), F32)
    for s in range(n_strips):
        dlt = cz_ref[s] - mu
        sq = sq + jnp.sum(dlt * dlt, axis=-1, keepdims=True)
    rstd = lax.rsqrt(sq / (n_strips * LANES) + EPS)
    for s in range(n_strips):
        ls = slice(s * LANES, (s + 1) * LANES)
        y = (cz_ref[s] - mu) * rstd * lng_ref[:, ls] + lnb_ref[:, ls]
        h_ref[:, ls] = _silu(y).astype(BF16)
    o_ref[...] = _dot(h_ref[...], w_ref[...])
    _gated_residual_rows(x_ref, o_ref, mod_ref, g_ref, rstd_ref, 1, 1.0)


def _conv_mix(cfg, z, w_dw, b_dw, ln_g, ln_b, w_pw2, t, mods, g6):
    n_tok, d = t.shape
    tm = CONV_TM
    assert cfg.ctx == tm
    hal = CONV_HALO
    n_strips = d // LANES
    hb = tm // hal
    n_hal = n_tok // hal
    row = _mod_row(cfg, tm)
    return pl.pallas_call(
        functools.partial(_conv_body, lat_tiles=cfg.n_lat // tm, per_seq=cfg.seq // tm),
        grid=(n_tok // tm,),
        in_specs=[
            pl.BlockSpec((hal, d), lambda i: (jnp.maximum(i * hb - 1, 0), 0)),
            pl.BlockSpec((tm, d), lambda i: (i, 0)),
            pl.BlockSpec((hal, d), lambda i: (jnp.minimum((i + 1) * hb, n_hal - 1), 0)),
            pl.BlockSpec((n_strips, 32, LANES), lambda i: (0, 0, 0)),
            pl.BlockSpec((n_strips, 1, LANES), lambda i: (0, 0, 0)),
            pl.BlockSpec((1, d), lambda i: (0, 0)),
            pl.BlockSpec((1, d), lambda i: (0, 0)),
            pl.BlockSpec((d, d), lambda i: (0, 0)),
            pl.BlockSpec((tm, d), lambda i: (i, 0)),
            pl.BlockSpec((None, 9, d), lambda i: (row(i), 0, 0)),
            pl.BlockSpec((6, d), lambda i: (0, 0)),
        ],
        out_specs=pl.BlockSpec((tm, d), lambda i: (i, 0)),
        out_shape=jax.ShapeDtypeStruct((n_tok, d), F32),
        scratch_shapes=[pltpu.VMEM((n_strips, tm + 2 * hal, LANES), F32),
                        pltpu.VMEM((n_strips, tm, LANES), F32),
                        pltpu.VMEM((tm, d), BF16), pltpu.VMEM((tm, 1), F32)],
        compiler_params=_params("arbitrary"),
        name="conv_mix",
    )(z, z, z, w_dw, b_dw, ln_g, ln_b, w_pw2, t, mods, g6)


def _rope_tables(cfg, n_ident):
    quarter = RET_DK // 4
    pos = jnp.arange(cfg.seq, dtype=jnp.int32)
    rows = (pos // GRID_W).astype(F32)
    cols = (pos % GRID_W).astype(F32)
    inv = ROPE_BASE ** (-jnp.arange(quarter, dtype=F32) / quarter)
    ar = rows[:, None] * inv[None, :]
    ac = cols[:, None] * inv[None, :]
    cos = jnp.concatenate([jnp.cos(ar), jnp.cos(ar), jnp.cos(ac), jnp.cos(ac)], axis=1)
    sin = jnp.concatenate([-jnp.sin(ar), jnp.sin(ar), -jnp.sin(ac), jnp.sin(ac)], axis=1)
    cos = jnp.concatenate([cos, jnp.ones((n_ident, RET_DK), F32)], axis=0)
    sin = jnp.concatenate([sin, jnp.zeros((n_ident, RET_DK), F32)], axis=0)
    return cos, sin


def _forward(cfg, x, c, ctx, c_ctx, ada_w, ada_b, norm_g, ffn_w_in, ffn_w_out, ret_w_in, ret_w_out,
             ret_decay_logit, ret_gn_g, gmlp_w_in, gmlp_ln_g, gmlp_ln_b, gmlp_w_s, gmlp_b_s, gmlp_w_out,
             conv_w_pw1, conv_w_dw, conv_b_dw, conv_ln_g, conv_ln_b, conv_w_pw2):
    d = x.shape[-1]
    depth = ada_w.shape[0]
    t = jnp.concatenate([x.reshape(cfg.n_lat, d), ctx.reshape(cfg.n_ctx, d)], axis=0)
    cc = jnp.zeros((ADA_ROWS, d), F32).at[:cfg.batch].set(c).at[cfg.batch].set(c_ctx)
    mods_all = _ada_table(cc, ada_w, ada_b)
    ret_w_in_bf = ret_w_in.astype(BF16)
    ret_w_out_bf = ret_w_out.astype(BF16)
    rope = _rope_tables(cfg, cfg.n_ctx)

    for i in range(depth):
        kind = i % N_MIXERS
        inst = i // N_MIXERS
        last = i == depth - 1
        mods = mods_all[i]
        g6 = norm_g[i].reshape(6, d)
        n_after = cfg.n_lat if last else cfg.n_tok
        t = _ffn_sublayer(cfg, t, mods, g6, ffn_w_in, ffn_w_out, i, 0, 0, cfg.n_tok)
        if kind == 0:
            p, kt = _inproj_ret(cfg, t, mods, g6, ret_w_in_bf, inst, rope)
            log_g = jax.nn.log_sigmoid(ret_decay_logit[inst].astype(F32))
            dec = jnp.zeros((RET_HEADS, 8, LANES), F32).at[:, 0:2, :].set(
                jnp.broadcast_to(log_g.T[:, :, None], (RET_HEADS, 2, LANES)))
            a_lat, a_ctx = _ret_scan(cfg, p, kt, dec, ret_gn_g[inst].reshape(1, RET_V))
            t = _outproj(cfg, a_lat, a_ctx, ret_w_out_bf, inst, t, mods, g6, n_after)
        elif kind == 1:
            z = _inproj(cfg, "gelu", t, mods, g6, gmlp_w_in[inst].astype(BF16))
            b_s = jnp.broadcast_to(gmlp_b_s[inst][:, :, None], (GMLP_GROUPS, CHUNK, LANES))
            t = _gmlp_mix(cfg, z, gmlp_ln_g[inst].reshape(1, GMLP_E), gmlp_ln_b[inst].reshape(1, GMLP_E),
                          gmlp_w_s[inst].astype(BF16), b_s, gmlp_w_out[inst].astype(BF16), t, mods, g6)
        else:
            z = _inproj(cfg, "glu", t, mods, g6, conv_w_pw1[inst].astype(BF16))
            n_strips = d // LANES
            w_dw = jnp.zeros((32, d), F32).at[:CONV_K].set(conv_w_dw[inst])
            w_dw = w_dw.reshape(32, n_strips, LANES).transpose(1, 0, 2)
            t = _conv_mix(cfg, z, w_dw, conv_b_dw[inst].reshape(n_strips, 1, LANES),
                          conv_ln_g[inst].reshape(1, d), conv_ln_b[inst].reshape(1, d),
                          conv_w_pw2[inst].astype(BF16), t, mods, g6)
        t = _ffn_sublayer(cfg, t, mods, g6, ffn_w_in, ffn_w_out, i, 1, 2, n_after)
    return t[:cfg.n_lat].reshape(x.shape)


def kernel(x, c, ctx, c_ctx, ada_w, ada_b, norm_g, ffn_w_in, ffn_w_out, ret_w_in, ret_w_out, ret_decay_logit,
           ret_gn_g, gmlp_w_in, gmlp_ln_g, gmlp_ln_b, gmlp_w_s, gmlp_b_s, gmlp_w_out, conv_w_pw1, conv_w_dw,
           conv_b_dw, conv_ln_g, conv_ln_b, conv_w_pw2):
    cfg = _Cfg(batch=x.shape[0], seq=x.shape[1], ctx=ctx.shape[1])
    return _forward(cfg, x, c, ctx, c_ctx, ada_w, ada_b, norm_g, ffn_w_in, ffn_w_out, ret_w_in, ret_w_out,
                    ret_decay_logit, ret_gn_g, gmlp_w_in, gmlp_ln_g, gmlp_ln_b, gmlp_w_s, gmlp_b_s, gmlp_w_out,
                    conv_w_pw1, conv_w_dw, conv_b_dw, conv_ln_g, conv_ln_b, conv_w_pw2)
```

```python
import functools
from typing import NamedTuple

import jax
import jax.numpy as jnp
from jax import lax
from jax.experimental import pallas as pl
from jax.experimental.pallas import tpu as pltpu

F32 = jnp.float32
BF16 = jnp.bfloat16

D_MODEL = 2048
DEPTH = 4
GRID_W = 64
N_MIXERS = 3
D_FF = 5632
RET_HEADS = 8
RET_DK = D_MODEL // RET_HEADS
RET_DV = 2 * RET_DK
RET_QK = RET_HEADS * RET_DK
RET_V = RET_HEADS * RET_DV
RET_IN = 2 * RET_QK + 2 * RET_V
CHUNK = 128
ROPE_BASE = 10000.0
GMLP_GROUPS = 8
GMLP_E = 3 * D_MODEL
GMLP_GE = GMLP_E // GMLP_GROUPS
CONV_K = 31
CONV_HALO = 16
EPS = 1e-6
ADA_ROWS = 8
LANES = 128

V7X_VMEM_BYTES = 64 * 1024 * 1024
VMEM_LIMIT = V7X_VMEM_BYTES - 4 * 1024 * 1024


class _Cfg(NamedTuple):
    batch: int
    seq: int
    ctx: int

    @property
    def n_lat(self):
        return self.batch * self.seq

    @property
    def n_ctx(self):
        return self.batch * self.ctx

    @property
    def n_tok(self):
        return self.n_lat + self.n_ctx


def _params(*sem):
    return pltpu.CompilerParams(dimension_semantics=sem, vmem_limit_bytes=VMEM_LIMIT)


def _tile(cfg, want):
    tm = want
    while cfg.seq % tm or cfg.n_ctx % tm:
        tm //= 2
    return tm


def _mod_row(cfg, tm):
    lat_tiles = cfg.n_lat // tm
    per_seq = cfg.seq // tm
    return lambda i: jnp.where(i < lat_tiles, i // per_seq, cfg.batch)


NORM_ROWS = 16


def _modulate_rows(x_ref, mod_ref, g_ref, h_ref, sub):
    x = x_ref[...]
    xn = x * lax.rsqrt(jnp.mean(x * x, axis=-1, keepdims=True) + EPS)
    h = (xn * g_ref[2 * sub:2 * sub + 1, :]) * (1.0 + mod_ref[3 * sub + 1:3 * sub + 2, :])
    h_ref[...] = (h + mod_ref[3 * sub:3 * sub + 1, :]).astype(BF16)


def _gated_residual_rows(x_ref, o_ref, mod_ref, g_ref, rstd_ref, sub, weight):
    blocks = [slice(i, i + NORM_ROWS) for i in range(0, x_ref.shape[0], NORM_ROWS)]
    for rows in blocks:
        v = o_ref[rows, :]
        rstd_ref[rows, :] = lax.rsqrt(jnp.mean(v * v, axis=-1, keepdims=True) + EPS)
    for rows in blocks:
        yn = o_ref[rows, :] * rstd_ref[rows, :]
        gate = weight * mod_ref[3 * sub + 2:3 * sub + 3, :]
        o_ref[rows, :] = x_ref[rows, :] + gate * (yn * g_ref[2 * sub + 1:2 * sub + 2, :])


def _standardize(x):
    xc = x - jnp.mean(x, axis=-1, keepdims=True)
    return xc * lax.rsqrt(jnp.mean(xc * xc, axis=-1, keepdims=True) + EPS)


def _silu(x):
    return x * jax.nn.sigmoid(x)


def _gelu_tanh(x):
    return 0.5 * x * (1.0 + jnp.tanh(0.7978845608028654 * (x + 0.044715 * (x * x * x))))


def _dot(a, b):
    return jnp.dot(a, b, preferred_element_type=F32)


def _wdot(a, w_ref, cols=slice(None)):
    return _dot(a, w_ref[:, cols].astype(BF16))


def _ada_body(c_ref, w_ref, b_ref, o_ref):
    s = _silu(c_ref[...]).astype(BF16)
    o_ref[...] = _dot(s, w_ref[...].astype(BF16)) + b_ref[...]


def _ada_table(cc, ada_w, ada_b):
    depth, d, n = ada_w.shape
    tn = 1024
    out = pl.pallas_call(
        _ada_body,
        grid=(depth, n // tn),
        in_specs=[
            pl.BlockSpec((ADA_ROWS, d), lambda i, j: (0, 0)),
            pl.BlockSpec((None, d, tn), lambda i, j: (i, 0, j)),
            pl.BlockSpec((None, 1, tn), lambda i, j: (i, 0, j)),
        ],
        out_specs=pl.BlockSpec((None, ADA_ROWS, tn), lambda i, j: (i, 0, j)),
        out_shape=jax.ShapeDtypeStruct((depth, ADA_ROWS, n), F32),
        compiler_params=_params("arbitrary", "arbitrary"),
        name="ada_table",
    )(cc, ada_w, ada_b.reshape(depth, 1, n))
    return out.reshape(depth, ADA_ROWS, n // d, d)


def _ffn_chunk(h_ref, o_ref, wa, wb, wo):
    h = h_ref[...]
    act = (_silu(_dot(h, wa)) * _dot(h, wb)).astype(BF16)
    o_ref[...] += _dot(act, wo)


def _ffn_first_body(x_ref, mod_ref, g_ref, wa32_ref, wb32_ref, wo32_ref, o_ref, wa_ref, wb_ref, wo_ref,
                    h_ref, rstd_ref, *, sub, n_ff):
    j = pl.program_id(0)

    @pl.when(j == 0)
    def _():
        _modulate_rows(x_ref, mod_ref, g_ref, h_ref, sub)
        o_ref[...] = jnp.zeros_like(o_ref)

    wa_ref[...] = wa32_ref[...].astype(BF16)
    wb_ref[...] = wb32_ref[...].astype(BF16)
    wo_ref[...] = wo32_ref[...].astype(BF16)
    _ffn_chunk(h_ref, o_ref, wa_ref[...], wb_ref[...], wo_ref[...])

    @pl.when(j == n_ff - 1)
    def _():
        _gated_residual_rows(x_ref, o_ref, mod_ref, g_ref, rstd_ref, sub, 0.5)


def _ffn_body(x_ref, mod_ref, g_ref, wa_ref, wb_ref, wo_ref, first_ref, o_ref, h_ref, rstd_ref, *, sub, n_ff):
    i = pl.program_id(0)
    j = pl.program_id(1)

    @pl.when(jnp.logical_and(i == 0, j == 0))
    def _():
        o_ref[...] = first_ref[...]

    @pl.when(i > 0)
    def _():
        @pl.when(j == 0)
        def _():
            _modulate_rows(x_ref, mod_ref, g_ref, h_ref, sub)
            o_ref[...] = jnp.zeros_like(o_ref)

        _ffn_chunk(h_ref, o_ref, wa_ref[...], wb_ref[...], wo_ref[...])

        @pl.when(j == n_ff - 1)
        def _():
            _gated_residual_rows(x_ref, o_ref, mod_ref, g_ref, rstd_ref, sub, 0.5)


def _ffn_sublayer(cfg, t, mods, g6, w_in, w_out, layer, which, sub, n_rows, tm=512, fc=512, fc_first=256):
    d = t.shape[1]
    tm = _tile(cfg, tm)
    f = w_out.shape[2]
    row = _mod_row(cfg, tm)
    scratch = [pltpu.VMEM((tm, d), BF16), pltpu.VMEM((tm, 1), F32)]

    n1 = f // fc_first
    first, wa, wb, wo = pl.pallas_call(
        functools.partial(_ffn_first_body, sub=sub, n_ff=n1),
        grid=(n1,),
        in_specs=[
            pl.BlockSpec((tm, d), lambda j: (0, 0)),
            pl.BlockSpec((None, 9, d), lambda j: (0, 0, 0)),
            pl.BlockSpec((6, d), lambda j: (0, 0)),
            pl.BlockSpec((None, None, d, fc_first), lambda j: (layer, which, 0, j)),
            pl.BlockSpec((None, None, d, fc_first), lambda j: (layer, which, 0, n1 + j)),
            pl.BlockSpec((None, None, fc_first, d), lambda j: (layer, which, j, 0)),
        ],
        out_specs=[pl.BlockSpec((tm, d), lambda j: (0, 0)),
                   pl.BlockSpec((d, fc_first), lambda j: (0, j)),
                   pl.BlockSpec((d, fc_first), lambda j: (0, j)),
                   pl.BlockSpec((fc_first, d), lambda j: (j, 0))],
        out_shape=[jax.ShapeDtypeStruct((tm, d), F32), jax.ShapeDtypeStruct((d, f), BF16),
                   jax.ShapeDtypeStruct((d, f), BF16), jax.ShapeDtypeStruct((f, d), BF16)],
        scratch_shapes=scratch,
        compiler_params=_params("arbitrary"),
        name="ffn_first_tile",
    )(t, mods, g6, w_in, w_in, w_out)

    n_ff = f // fc
    chunk = lambda i, j: jnp.where(i == 0, 0, j)
    return pl.pallas_call(
        functools.partial(_ffn_body, sub=sub, n_ff=n_ff),
        grid=(n_rows // tm, n_ff),
        in_specs=[
            pl.BlockSpec((tm, d), lambda i, j: (i, 0)),
            pl.BlockSpec((None, 9, d), lambda i, j: (row(i), 0, 0)),
            pl.BlockSpec((6, d), lambda i, j: (0, 0)),
            pl.BlockSpec((d, fc), lambda i, j: (0, chunk(i, j))),
            pl.BlockSpec((d, fc), lambda i, j: (0, chunk(i, j))),
            pl.BlockSpec((fc, d), lambda i, j: (chunk(i, j), 0)),
            pl.BlockSpec((tm, d), lambda i, j: (0, 0)),
        ],
        out_specs=pl.BlockSpec((tm, d), lambda i, j: (i, 0)),
        out_shape=jax.ShapeDtypeStruct((n_rows, d), F32),
        scratch_shapes=scratch,
        compiler_params=_params("arbitrary", "arbitrary"),
        name="ffn_sublayer",
    )(t, mods, g6, wa, wb, wo, first)


def _inproj_prologue(x_ref, mod_ref, g_ref, h_ref):
    @pl.when(pl.program_id(1) == 0)
    def _():
        _modulate_rows(x_ref, mod_ref, g_ref, h_ref, 1)


def _inproj_ret_body(x_ref, mod_ref, g_ref, cos_ref, sin_ref, w_ref, o_ref, kt_ref, h_ref, *, n_q, n_k):
    _inproj_prologue(x_ref, mod_ref, g_ref, h_ref)
    j = pl.program_id(1)
    tm, tn = o_ref.shape
    h = h_ref[...]

    def rotated_head(hd, scale):
        a = _wdot(h, w_ref, slice(hd * RET_DK, (hd + 1) * RET_DK))
        swapped = jnp.concatenate(
            [pltpu.roll(a[:, g * LANES:(g + 1) * LANES], LANES // 2, 1) for g in range(RET_DK // LANES)], axis=1)
        return (a * cos_ref[...] + swapped * sin_ref[...]) * scale

    @pl.when(j < n_q)
    def _():
        for hd in range(tn // RET_DK):
            o_ref[:, hd * RET_DK:(hd + 1) * RET_DK] = rotated_head(hd, 1.0).astype(BF16)

    @pl.when(jnp.logical_and(j >= n_q, j < n_q + n_k))
    def _():
        for hd in range(tn // RET_DK):
            r = rotated_head(hd, RET_DK ** -0.5)
            for ch in range(tm // CHUNK):
                kt_ref[ch, hd * RET_DK:(hd + 1) * RET_DK, :] = r[ch * CHUNK:(ch + 1) * CHUNK, :].T.astype(BF16)

    @pl.when(j >= n_q + n_k)
    def _():
        for hd in range(tn // RET_DK):
            cols = slice(hd * RET_DK, (hd + 1) * RET_DK)
            o_ref[:, cols] = _wdot(h, w_ref, cols).astype(BF16)


def _inproj_ret(cfg, t, mods, g6, w_all, inst, rope, tm=1024, tn=512):
    n_tok, d = t.shape
    n_proj = w_all.shape[2]
    tm = _tile(cfg, tm)
    row = _mod_row(cfg, tm)
    lat_tiles = cfg.n_lat // tm
    per_seq = cfg.seq // tm
    trow = lambda i: jnp.where(i < lat_tiles, i % per_seq, per_seq + i - lat_tiles)
    n_q = RET_QK // tn
    n_k = RET_QK // tn
    pcol = lambda j: jnp.where(j < n_q, j, jnp.maximum(j - n_k, n_q - 1))
    kcol = lambda j: jnp.clip(j - n_q, 0, n_k - 1)
    return pl.pallas_call(
        functools.partial(_inproj_ret_body, n_q=n_q, n_k=n_k),
        grid=(n_tok // tm, n_proj // tn),
        in_specs=[
            pl.BlockSpec((tm, d), lambda i, j: (i, 0)),
            pl.BlockSpec((None, 9, d), lambda i, j: (row(i), 0, 0)),
            pl.BlockSpec((6, d), lambda i, j: (0, 0)),
            pl.BlockSpec((tm, RET_DK), lambda i, j: (trow(i), 0)),
            pl.BlockSpec((tm, RET_DK), lambda i, j: (trow(i), 0)),
            pl.BlockSpec((None, d, tn), lambda i, j: (inst, 0, j)),
        ],
        out_specs=[pl.BlockSpec((tm, tn), lambda i, j: (i, pcol(j))),
                   pl.BlockSpec((tm // CHUNK, tn, CHUNK), lambda i, j: (i, kcol(j), 0))],
        out_shape=[jax.ShapeDtypeStruct((n_tok, n_proj - RET_QK), BF16),
                   jax.ShapeDtypeStruct((n_tok // CHUNK, RET_QK, CHUNK), BF16)],
        scratch_shapes=[pltpu.VMEM((tm, d), BF16)],
        compiler_params=_params("arbitrary", "arbitrary"),
        name="inproj_ret",
    )(t, mods, g6, rope[0], rope[1], w_all)


INPROJ_PIECE = 256


def _inproj_gelu_body(x_ref, mod_ref, g_ref, w_ref, o_ref, h_ref):
    _inproj_prologue(x_ref, mod_ref, g_ref, h_ref)
    h = h_ref[...]
    for s in range(o_ref.shape[1] // INPROJ_PIECE):
        cols = slice(s * INPROJ_PIECE, (s + 1) * INPROJ_PIECE)
        o_ref[:, cols] = _gelu_tanh(_wdot(h, w_ref, cols)).astype(BF16)


def _inproj_glu_body(x_ref, mod_ref, g_ref, wa_ref, wb_ref, o_ref, h_ref):
    _inproj_prologue(x_ref, mod_ref, g_ref, h_ref)
    h = h_ref[...]
    for s in range(o_ref.shape[1] // INPROJ_PIECE):
        cols = slice(s * INPROJ_PIECE, (s + 1) * INPROJ_PIECE)
        o_ref[:, cols] = (_wdot(h, wa_ref, cols) * jax.nn.sigmoid(_wdot(h, wb_ref, cols))).astype(BF16)


def _inproj(cfg, kind, t, mods, g6, w, tm=1024):
    tn = 1024 if kind == "gelu" else 512
    n_tok, d = t.shape
    tm = _tile(cfg, tm)
    row = _mod_row(cfg, tm)
    common = [
        pl.BlockSpec((tm, d), lambda i, j: (i, 0)),
        pl.BlockSpec((None, 9, d), lambda i, j: (row(i), 0, 0)),
        pl.BlockSpec((6, d), lambda i, j: (0, 0)),
    ]
    if kind == "gelu":
        n_out = w.shape[1]
        body = _inproj_gelu_body
        specs = common + [pl.BlockSpec((d, tn), lambda i, j: (0, j))]
        args = (t, mods, g6, w)
    else:
        n_out = w.shape[1] // 2
        nb = n_out // tn
        body = _inproj_glu_body
        specs = common + [pl.BlockSpec((d, tn), lambda i, j: (0, j)),
                          pl.BlockSpec((d, tn), lambda i, j: (0, nb + j))]
        args = (t, mods, g6, w, w)
    return pl.pallas_call(
        body,
        grid=(n_tok // tm, n_out // tn),
        in_specs=specs,
        out_specs=pl.BlockSpec((tm, tn), lambda i, j: (i, j)),
        out_shape=jax.ShapeDtypeStruct((n_tok, n_out), BF16),
        scratch_shapes=[pltpu.VMEM((tm, d), BF16)],
        compiler_params=_params("arbitrary", "arbitrary"),
        name="inproj_" + kind,
    )(*args)


SCAN_CHUNK = 2 * CHUNK


def _ret_body(dec_ref, gn_ref, ql_ref, ktl_ref, vl_ref, gl_ref, qc_ref, ktc_ref, vc_ref, gc_ref, ol_ref, oc_ref,
              sf_ref, sb_ref, af_ref, ab_ref, *, n_lat_chunks, n_ctx_chunks):
    c = SCAN_CHUNK
    sub = c // CHUNK
    lgf = dec_ref[0:1, 0:1]
    lgb = dec_ref[1:2, 0:1]
    diff = (lax.broadcasted_iota(jnp.int32, (c, c), 0) - lax.broadcasted_iota(jnp.int32, (c, c), 1)).astype(F32)
    dmat = jnp.where(diff >= 0.0, jnp.exp(jnp.maximum(diff, 0.0) * lgf), jnp.exp(jnp.maximum(-diff, 0.0) * lgb))
    pos = lax.broadcasted_iota(jnp.int32, (c, 1), 0).astype(F32)
    lane = lax.broadcasted_iota(jnp.int32, (1, c), 1).astype(F32)
    qdec_f = jnp.exp((pos + 1.0) * lgf)
    kdec_f = jnp.exp((c - 1.0 - lane) * lgf)
    cdec_f = jnp.exp(c * lgf)
    qdec_b = jnp.exp((c - pos) * lgb)
    kdec_b = jnp.exp(lane * lgb)
    cdec_b = jnp.exp(c * lgb)

    def kt_chunk(kt_ref, i):
        return jnp.concatenate([kt_ref[i * sub + s] for s in range(sub)], axis=1)

    def pair(q_ref, kt_ref, v_ref, cf, rf, cb, rb):
        q = q_ref[rf, :]
        kt = kt_chunk(kt_ref, cf)
        v = v_ref[rf, :]
        a = _dot(q, kt) * dmat
        af_ref[rf, :] = _dot(a.astype(BF16), v) + _dot(q, sf_ref[...].astype(BF16)) * qdec_f
        sf_ref[...] = sf_ref[...] * cdec_f + _dot((kt.astype(F32) * kdec_f).astype(BF16), v)
        q = q_ref[rb, :]
        kt = kt_chunk(kt_ref, cb)
        v = v_ref[rb, :]
        ab_ref[rb, :] = _dot(q, sb_ref[...].astype(BF16)) * qdec_b
        sb_ref[...] = sb_ref[...] * cdec_b + _dot((kt.astype(F32) * kdec_b).astype(BF16), v)

    def finish(g_ref, o_ref, rows):
        o = af_ref[rows, :] + ab_ref[rows, :]
        o_ref[rows, :] = (_silu(g_ref[rows, :].astype(F32)) * (_standardize(o) * gn_ref[...])).astype(BF16)

    def chunk_rows(i):
        return pl.ds(pl.multiple_of(i * c, c), c)

    sf_ref[...] = jnp.zeros_like(sf_ref)
    sb_ref[...] = jnp.zeros_like(sb_ref)
    for i in range(n_ctx_chunks):
        ib = n_ctx_chunks - 1 - i
        pair(qc_ref, ktc_ref, vc_ref, i, slice(i * c, (i + 1) * c), ib, slice(ib * c, (ib + 1) * c))
    for i in range(n_ctx_chunks):
        finish(gc_ref, oc_ref, slice(i * c, (i + 1) * c))

    def scan_step(i, carry):
        ib = n_lat_chunks - 1 - i
        pair(ql_ref, ktl_ref, vl_ref, i, chunk_rows(i), ib, chunk_rows(ib))
        return carry

    lax.fori_loop(0, n_lat_chunks, scan_step, 0)

    def finish_step(i, carry):
        finish(gl_ref, ol_ref, chunk_rows(i))
        return carry

    lax.fori_loop(0, n_lat_chunks, finish_step, 0)


def _ret_scan(cfg, p, kt, dec, gn_g):
    assert cfg.seq % SCAN_CHUNK == 0 and cfg.ctx % SCAN_CHUNK == 0
    v_off = RET_QK // RET_DV
    g_off = v_off + RET_V // RET_DV
    ctx0 = cfg.n_lat // cfg.ctx
    return pl.pallas_call(
        functools.partial(_ret_body, n_lat_chunks=cfg.seq // SCAN_CHUNK, n_ctx_chunks=cfg.ctx // SCAN_CHUNK),
        grid=(cfg.batch, RET_HEADS),
        in_specs=[
            pl.BlockSpec((None, 8, LANES), lambda b, h: (h, 0, 0)),
            pl.BlockSpec((1, RET_DV), lambda b, h: (0, h)),
            pl.BlockSpec((cfg.seq, RET_DK), lambda b, h: (b, h)),
            pl.BlockSpec((cfg.seq // CHUNK, RET_DK, CHUNK), lambda b, h: (b, h, 0)),
            pl.BlockSpec((cfg.seq, RET_DV), lambda b, h: (b, v_off + h)),
            pl.BlockSpec((cfg.seq, RET_DV), lambda b, h: (b, g_off + h)),
            pl.BlockSpec((cfg.ctx, RET_DK), lambda b, h: (ctx0 + b, h)),
            pl.BlockSpec((cfg.ctx // CHUNK, RET_DK, CHUNK), lambda b, h: (ctx0 + b, h, 0)),
            pl.BlockSpec((cfg.ctx, RET_DV), lambda b, h: (ctx0 + b, v_off + h)),
            pl.BlockSpec((cfg.ctx, RET_DV), lambda b, h: (ctx0 + b, g_off + h)),
        ],
        out_specs=[pl.BlockSpec((cfg.seq, RET_DV), lambda b, h: (b, h)),
                   pl.BlockSpec((cfg.ctx, RET_DV), lambda b, h: (b, h))],
        out_shape=[jax.ShapeDtypeStruct((cfg.n_lat, RET_V), BF16),
                   jax.ShapeDtypeStruct((cfg.n_ctx, RET_V), BF16)],
        scratch_shapes=[pltpu.VMEM((RET_DK, RET_DV), F32), pltpu.VMEM((RET_DK, RET_DV), F32),
                        pltpu.VMEM((cfg.seq, RET_DV), F32), pltpu.VMEM((cfg.seq, RET_DV), F32)],
        compiler_params=_params("arbitrary", "arbitrary"),
        name="ret_scan",
    )(dec, gn_g, p, kt, p, p, p, kt, p, p)


def _outproj_body(al_ref, ac_ref, w_ref, x_ref, mod_ref, g_ref, o_ref, rstd_ref, *, n_k, lat_tiles):
    i = pl.program_id(0)
    kk = pl.program_id(1)

    @pl.when(kk == 0)
    def _():
        o_ref[...] = jnp.zeros_like(o_ref)

    @pl.when(i < lat_tiles)
    def _():
        o_ref[...] += _wdot(al_ref[...], w_ref)

    @pl.when(i >= lat_tiles)
    def _():
        o_ref[...] += _wdot(ac_ref[...], w_ref)

    @pl.when(kk == n_k - 1)
    def _():
        _gated_residual_rows(x_ref, o_ref, mod_ref, g_ref, rstd_ref, 1, 1.0)


def _outproj(cfg, a_lat, a_ctx, w_all, inst, t, mods, g6, n_rows, tm=1024, kc=512):
    d = t.shape[1]
    tm = _tile(cfg, tm)
    n_k = a_lat.shape[1] // kc
    row = _mod_row(cfg, tm)
    lat_tiles = cfg.n_lat // tm
    lat_idx = lambda i, k: (jnp.minimum(i, lat_tiles - 1), jnp.where(i < lat_tiles, k, n_k - 1))
    ctx_idx = lambda i, k: (jnp.maximum(i - lat_tiles, 0), jnp.where(i < lat_tiles, 0, k))
    return pl.pallas_call(
        functools.partial(_outproj_body, n_k=n_k, lat_tiles=lat_tiles),
        grid=(n_rows // tm, n_k),
        in_specs=[
            pl.BlockSpec((tm, kc), lat_idx),
            pl.BlockSpec((tm, kc), ctx_idx),
            pl.BlockSpec((None, kc, d), lambda i, k: (inst, k, 0)),
            pl.BlockSpec((tm, d), lambda i, k: (i, 0)),
            pl.BlockSpec((None, 9, d), lambda i, k: (row(i), 0, 0)),
            pl.BlockSpec((6, d), lambda i, k: (0, 0)),
        ],
        out_specs=pl.BlockSpec((tm, d), lambda i, k: (i, 0)),
        out_shape=jax.ShapeDtypeStruct((n_rows, d), F32),
        scratch_shapes=[pltpu.VMEM((tm, 1), F32)],
        compiler_params=_params("arbitrary", "arbitrary"),
        name="outproj",
    )(a_lat, a_ctx, w_all, t, mods, g6)


def _gmlp_body(vfull_ref, v_ref, u_ref, lng_ref, lnb_ref, ws_ref, bs_ref, w_ref, x_ref, mod_ref, g_ref,
               o_ref, mean_ref, rstd_ref, gated_ref, *, n_groups):
    grp = pl.program_id(1)
    tm = v_ref.shape[0]

    @pl.when(grp == 0)
    def _():
        v = vfull_ref[...].astype(F32)
        mu = jnp.mean(v, axis=-1, keepdims=True)
        vc = v - mu
        mean_ref[...] = mu
        rstd_ref[...] = lax.rsqrt(jnp.mean(vc * vc, axis=-1, keepdims=True) + EPS)
        o_ref[...] = jnp.zeros_like(o_ref)

    vn = ((v_ref[...].astype(F32) - mean_ref[...]) * rstd_ref[...] * lng_ref[...] + lnb_ref[...]).astype(BF16)
    ws = ws_ref[...]
    bias = bs_ref[:, 0:1]
    for i in range(tm // CHUNK):
        sl = slice(i * CHUNK, (i + 1) * CHUNK)
        mixed = _dot(ws, vn[sl, :]) + bias
        gated_ref[sl, :] = (u_ref[sl, :].astype(F32) * mixed).astype(BF16)
    o_ref[...] += _wdot(gated_ref[...], w_ref)

    @pl.when(grp == n_groups - 1)
    def _():
        _gated_residual_rows(x_ref, o_ref, mod_ref, g_ref, rstd_ref, 1, 1.0)


def _gmlp_mix(cfg, z, ln_g, ln_b, w_s, b_s, w_out, t, mods, g6, tm=512):
    n_tok, d = t.shape
    tm = _tile(cfg, tm)
    ge = GMLP_GE
    row = _mod_row(cfg, tm)
    return pl.pallas_call(
        functools.partial(_gmlp_body, n_groups=GMLP_GROUPS),
        grid=(n_tok // tm, GMLP_GROUPS),
        in_specs=[
            pl.BlockSpec((tm, GMLP_E), lambda i, g: (i, 1)),
            pl.BlockSpec((tm, ge), lambda i, g: (i, GMLP_GROUPS + g)),
            pl.BlockSpec((tm, ge), lambda i, g: (i, g)),
            pl.BlockSpec((1, ge), lambda i, g: (0, g)),
            pl.BlockSpec((1, ge), lambda i, g: (0, g)),
            pl.BlockSpec((None, CHUNK, CHUNK), lambda i, g: (g, 0, 0)),
            pl.BlockSpec((None, CHUNK, LANES), lambda i, g: (g, 0, 0)),
            pl.BlockSpec((ge, d), lambda i, g: (g, 0)),
            pl.BlockSpec((tm, d), lambda i, g: (i, 0)),
            pl.BlockSpec((None, 9, d), lambda i, g: (row(i), 0, 0)),
            pl.BlockSpec((6, d), lambda i, g: (0, 0)),
        ],
        out_specs=pl.BlockSpec((tm, d), lambda i, g: (i, 0)),
        out_shape=jax.ShapeDtypeStruct((n_tok, d), F32),
        scratch_shapes=[pltpu.VMEM((tm, 1), F32), pltpu.VMEM((tm, 1), F32), pltpu.VMEM((tm, ge), BF16)],
        compiler_params=_params("arbitrary", "arbitrary"),
        name="gmlp_mix",
    )(z, z, z, ln_g, ln_b, w_s, b_s, w_out, t, mods, g6)


CONV_TM = 256


def _conv_body(prev_ref, cur_ref, next_ref, wdw_ref, bdw_ref, lng_ref, lnb_ref, w_ref, x_ref, mod_ref, g_ref,
               o_ref, zs_ref, cz_ref, h_ref, rstd_ref, *, lat_tiles, per_seq):
    i = pl.program_id(0)
    tm = cur_ref.shape[0]
    n_strips = cur_ref.shape[1] // LANES
    hal = CONV_HALO
    pad = CONV_K // 2
    is_ctx = i >= lat_tiles
    first = jnp.logical_or(is_ctx, i % per_seq == 0)
    last = jnp.logical_or(is_ctx, i % per_seq == per_seq - 1)
    pscale = jnp.where(first, 0.0, 1.0)
    nscale = jnp.where(last, 0.0, 1.0)
    for s in range(n_strips):
        ls = slice(s * LANES, (s + 1) * LANES)
        zs_ref[s, 0:hal, :] = prev_ref[:, ls].astype(F32) * pscale
        zs_ref[s, hal:hal + tm, :] = cur_ref[:, ls].astype(F32)
        zs_ref[s, hal + tm:hal + tm + hal, :] = next_ref[:, ls].astype(F32) * nscale

    def strip(s, carry):
        acc = jnp.zeros((tm, LANES), F32) + bdw_ref[s]
        for k in range(CONV_K):
            acc = acc + wdw_ref[s, k:k + 1, :] * zs_ref[s, hal - pad + k:hal - pad + k + tm, :]
        cz_ref[s] = acc
        return carry

    lax.fori_loop(0, n_strips, strip, 0)

    tot = jnp.zeros((tm, 1), F32)
    for s in range(n_strips):
        tot = tot + jnp.sum(cz_ref[s], axis=-1, keepdims=True)
    mu = tot / (n_strips * LANES)
    sq = jnp.zeros((tm, 1), F32)
    for s in range(n_strips):
        dlt = cz_ref[s] - mu
        sq = sq + jnp.sum(dlt * dlt, axis=-1, keepdims=True)
    rstd = lax.rsqrt(sq / (n_strips * LANES) + EPS)
    for s in range(n_strips):
        ls = slice(s * LANES, (s + 1) * LANES)
        y = (cz_ref[s] - mu) * rstd * lng_ref[:, ls] + lnb_ref[:, ls]
        h_ref[:, ls] = _silu(y).astype(BF16)
    o_ref[...] = _dot(h_ref[...], w_ref[...])
    _gated_residual_rows(x_ref, o_ref, mod_ref, g_ref, rstd_ref, 1, 1.0)


def _conv_mix(cfg, z, w_dw, b_dw, ln_g, ln_b, w_pw2, t, mods, g6):
    n_tok, d = t.shape
    tm = CONV_TM
    assert cfg.ctx == tm
    hal = CONV_HALO
    n_strips = d // LANES
    hb = tm // hal
    n_hal = n_tok // hal
    row = _mod_row(cfg, tm)
    return pl.pallas_call(
        functools.partial(_conv_body, lat_tiles=cfg.n_lat // tm, per_seq=cfg.seq // tm),
        grid=(n_tok // tm,),
        in_specs=[
            pl.BlockSpec((hal, d), lambda i: (jnp.maximum(i * hb - 1, 0), 0)),
            pl.BlockSpec((tm, d), lambda i: (i, 0)),
            pl.BlockSpec((hal, d), lambda i: (jnp.minimum((i + 1) * hb, n_hal - 1), 0)),
            pl.BlockSpec((n_strips, 32, LANES), lambda i: (0, 0, 0)),
            pl.BlockSpec((n_strips, 1, LANES), lambda i: (0, 0, 0)),
            pl.BlockSpec((1, d), lambda i: (0, 0)),
            pl.BlockSpec((1, d), lambda i: (0, 0)),
            pl.BlockSpec((d, d), lambda i: (0, 0)),
            pl.BlockSpec((tm, d), lambda i: (i, 0)),
            pl.BlockSpec((None, 9, d), lambda i: (row(i), 0, 0)),
            pl.BlockSpec((6, d), lambda i: (0, 0)),
        ],
        out_specs=pl.BlockSpec((tm, d), lambda i: (i, 0)),
        out_shape=jax.ShapeDtypeStruct((n_tok, d), F32),
        scratch_shapes=[pltpu.VMEM((n_strips, tm + 2 * hal, LANES), F32),
                        pltpu.VMEM((n_strips, tm, LANES), F32),
                        pltpu.VMEM((tm, d), BF16), pltpu.VMEM((tm, 1), F32)],
        compiler_params=_params("arbitrary"),
        name="conv_mix",
    )(z, z, z, w_dw, b_dw, ln_g, ln_b, w_pw2, t, mods, g6)


def _rope_tables(cfg, n_ident):
    quarter = RET_DK // 4
    pos = jnp.arange(cfg.seq, dtype=jnp.int32)
    rows = (pos // GRID_W).astype(F32)
    cols = (pos % GRID_W).astype(F32)
    inv = ROPE_BASE ** (-jnp.arange(quarter, dtype=F32) / quarter)
    ar = rows[:, None] * inv[None, :]
    ac = cols[:, None] * inv[None, :]
    cos = jnp.concatenate([jnp.cos(ar), jnp.cos(ar), jnp.cos(ac), jnp.cos(ac)], axis=1)
    sin = jnp.concatenate([-jnp.sin(ar), jnp.sin(ar), -jnp.sin(ac), jnp.sin(ac)], axis=1)
    cos = jnp.concatenate([cos, jnp.ones((n_ident, RET_DK), F32)], axis=0)
    sin = jnp.concatenate([sin, jnp.zeros((n_ident, RET_DK), F32)], axis=0)
    return cos, sin


def _forward(cfg, x, c, ctx, c_ctx, ada_w, ada_b, norm_g, ffn_w_in, ffn_w_out, ret_w_in, ret_w_out,
             ret_decay_logit, ret_gn_g, gmlp_w_in, gmlp_ln_g, gmlp_ln_b, gmlp_w_s, gmlp_b_s, gmlp_w_out,
             conv_w_pw1, conv_w_dw, conv_b_dw, conv_ln_g, conv_ln_b, conv_w_pw2):
    d = x.shape[-1]
    depth = ada_w.shape[0]
    t = jnp.concatenate([x.reshape(cfg.n_lat, d), ctx.reshape(cfg.n_ctx, d)], axis=0)
    cc = jnp.zeros((ADA_ROWS, d), F32).at[:cfg.batch].set(c).at[cfg.batch].set(c_ctx)
    mods_all = _ada_table(cc, ada_w, ada_b)
    rope = _rope_tables(cfg, cfg.n_ctx)

    for i in range(depth):
        kind = i % N_MIXERS
        inst = i // N_MIXERS
        last = i == depth - 1
        mods = mods_all[i]
        g6 = norm_g[i].reshape(6, d)
        n_after = cfg.n_lat if last else cfg.n_tok
        t = _ffn_sublayer(cfg, t, mods, g6, ffn_w_in, ffn_w_out, i, 0, 0, cfg.n_tok)
        if kind == 0:
            p, kt = _inproj_ret(cfg, t, mods, g6, ret_w_in, inst, rope)
            log_g = jax.nn.log_sigmoid(ret_decay_logit[inst].astype(F32))
            dec = jnp.zeros((RET_HEADS, 8, LANES), F32).at[:, 0:2, :].set(
                jnp.broadcast_to(log_g.T[:, :, None], (RET_HEADS, 2, LANES)))
            a_lat, a_ctx = _ret_scan(cfg, p, kt, dec, ret_gn_g[inst].reshape(1, RET_V))
            t = _outproj(cfg, a_lat, a_ctx, ret_w_out, inst, t, mods, g6, n_after)
        elif kind == 1:
            z = _inproj(cfg, "gelu", t, mods, g6, gmlp_w_in[inst])
            b_s = jnp.broadcast_to(gmlp_b_s[inst][:, :, None], (GMLP_GROUPS, CHUNK, LANES))
            t = _gmlp_mix(cfg, z, gmlp_ln_g[inst].reshape(1, GMLP_E), gmlp_ln_b[inst].reshape(1, GMLP_E),
                          gmlp_w_s[inst].astype(BF16), b_s, gmlp_w_out[inst], t, mods, g6)
        else:
            z = _inproj(cfg, "glu", t, mods, g6, conv_w_pw1[inst])
            n_strips = d // LANES
            w_dw = jnp.zeros((32, d), F32).at[:CONV_K].set(conv_w_dw[inst])
            w_dw = w_dw.reshape(32, n_strips, LANES).transpose(1, 0, 2)
            t = _conv_mix(cfg, z, w_dw, conv_b_dw[inst].reshape(n_strips, 1, LANES),
                          conv_ln_g[inst].reshape(1, d), conv_ln_b[inst].reshape(1, d),
                          conv_w_pw2[inst].astype(BF16), t, mods, g6)
        t = _ffn_sublayer(cfg, t, mods, g6, ffn_w_in, ffn_w_out, i, 1, 2, n_after)
    return t[:cfg.n_lat].reshape(x.shape)


def kernel(x, c, ctx, c_ctx, ada_w, ada_b, norm_g, ffn_w_in, ffn_w_out, ret_w_in, ret_w_out, ret_decay_logit,
           ret_gn_g, gmlp_w_in, gmlp_ln_g, gmlp_ln_b, gmlp_w_s, gmlp_b_s, gmlp_w_out, conv_w_pw1, conv_w_dw,
           conv_b_dw, conv_ln_g, conv_ln_b, conv_w_pw2):
    cfg = _Cfg(batch=x.shape[0], seq=x.shape[1], ctx=ctx.shape[1])
    return _forward(cfg, x, c, ctx, c_ctx, ada_w, ada_b, norm_g, ffn_w_in, ffn_w_out, ret_w_in, ret_w_out,
                    ret_decay_logit, ret_gn_g, gmlp_w_in, gmlp_ln_g, gmlp_ln_b, gmlp_w_s, gmlp_b_s, gmlp_w_out,
                    conv_w_pw1, conv_w_dw, conv_b_dw, conv_ln_g, conv_ln_b, conv_w_pw2)
```

```python
import functools
from typing import NamedTuple

import jax
import jax.numpy as jnp
from jax import lax
from jax.experimental import pallas as pl
from jax.experimental.pallas import tpu as pltpu

F32 = jnp.float32
BF16 = jnp.bfloat16

D_MODEL = 2048
DEPTH = 4
GRID_W = 64
N_MIXERS = 3
D_FF = 5632
RET_HEADS = 8
RET_DK = D_MODEL // RET_HEADS
RET_DV = 2 * RET_DK
RET_QK = RET_HEADS * RET_DK
RET_V = RET_HEADS * RET_DV
RET_IN = 2 * RET_QK + 2 * RET_V
CHUNK = 128
ROPE_BASE = 10000.0
GMLP_GROUPS = 8
GMLP_E = 3 * D_MODEL
GMLP_GE = GMLP_E // GMLP_GROUPS
CONV_K = 31
CONV_HALO = 16
EPS = 1e-6
ADA_ROWS = 8
LANES = 128

V7X_VMEM_BYTES = 64 * 1024 * 1024
VMEM_LIMIT = V7X_VMEM_BYTES - 4 * 1024 * 1024


class _Cfg(NamedTuple):
    batch: int
    seq: int
    ctx: int

    @property
    def n_lat(self):
        return self.batch * self.seq

    @property
    def n_ctx(self):
        return self.batch * self.ctx

    @property
    def n_tok(self):
        return self.n_lat + self.n_ctx


def _params(*sem):
    return pltpu.CompilerParams(dimension_semantics=sem, vmem_limit_bytes=VMEM_LIMIT)


def _tile(cfg, want):
    tm = want
    while cfg.seq % tm or cfg.n_ctx % tm:
        tm //= 2
    return tm


def _mod_row(cfg, tm):
    lat_tiles = cfg.n_lat // tm
    per_seq = cfg.seq // tm
    return lambda i: jnp.where(i < lat_tiles, i // per_seq, cfg.batch)


NORM_ROWS = 16


def _modulate_rows(x_ref, mod_ref, g_ref, h_ref, sub):
    x = x_ref[...]
    xn = x * lax.rsqrt(jnp.mean(x * x, axis=-1, keepdims=True) + EPS)
    h = (xn * g_ref[2 * sub:2 * sub + 1, :]) * (1.0 + mod_ref[3 * sub + 1:3 * sub + 2, :])
    h_ref[...] = (h + mod_ref[3 * sub:3 * sub + 1, :]).astype(BF16)


def _gated_residual_rows(x_ref, o_ref, mod_ref, g_ref, rstd_ref, sub, weight):
    blocks = [slice(i, i + NORM_ROWS) for i in range(0, x_ref.shape[0], NORM_ROWS)]
    for rows in blocks:
        v = o_ref[rows, :]
        rstd_ref[rows, :] = lax.rsqrt(jnp.mean(v * v, axis=-1, keepdims=True) + EPS)
    for rows in blocks:
        yn = o_ref[rows, :] * rstd_ref[rows, :]
        gate = weight * mod_ref[3 * sub + 2:3 * sub + 3, :]
        o_ref[rows, :] = x_ref[rows, :] + gate * (yn * g_ref[2 * sub + 1:2 * sub + 2, :])


def _standardize(x):
    xc = x - jnp.mean(x, axis=-1, keepdims=True)
    return xc * lax.rsqrt(jnp.mean(xc * xc, axis=-1, keepdims=True) + EPS)


def _silu(x):
    return x * jax.nn.sigmoid(x)


def _gelu_tanh(x):
    return 0.5 * x * (1.0 + jnp.tanh(0.7978845608028654 * (x + 0.044715 * (x * x * x))))


def _dot(a, b):
    return jnp.dot(a, b, preferred_element_type=F32)


def _wdot(a, w_ref, cols=slice(None)):
    return _dot(a, w_ref[:, cols].astype(BF16))


def _ada_body(c_ref, w_ref, b_ref, o_ref):
    s = _silu(c_ref[...]).astype(BF16)
    o_ref[...] = _dot(s, w_ref[...].astype(BF16)) + b_ref[...]


def _ada_table(cc, ada_w, ada_b):
    depth, d, n = ada_w.shape
    tn = 1024
    out = pl.pallas_call(
        _ada_body,
        grid=(depth, n // tn),
        in_specs=[
            pl.BlockSpec((ADA_ROWS, d), lambda i, j: (0, 0)),
            pl.BlockSpec((None, d, tn), lambda i, j: (i, 0, j)),
            pl.BlockSpec((None, 1, tn), lambda i, j: (i, 0, j)),
        ],
        out_specs=pl.BlockSpec((None, ADA_ROWS, tn), lambda i, j: (i, 0, j)),
        out_shape=jax.ShapeDtypeStruct((depth, ADA_ROWS, n), F32),
        compiler_params=_params("arbitrary", "arbitrary"),
        name="ada_table",
    )(cc, ada_w, ada_b.reshape(depth, 1, n))
    return out.reshape(depth, ADA_ROWS, n // d, d)


FIRST_TILES = 2


def _ffn_chunk(h_ref, o_ref, wa, wb, wo):
    h = h_ref[...]
    act = (_silu(_dot(h, wa)) * _dot(h, wb)).astype(BF16)
    o_ref[...] += _dot(act, wo)


def _ffn_first_body(x_ref, mod_ref, g_ref, wa32_ref, wb32_ref, wo32_ref, o_ref, wa_ref, wb_ref, wo_ref,
                    h_ref, rstd_ref, *, sub, n_ff):
    j = pl.program_id(0)

    @pl.when(j == 0)
    def _():
        _modulate_rows(x_ref, mod_ref, g_ref, h_ref, sub)
        o_ref[...] = jnp.zeros_like(o_ref)

    wa_ref[...] = wa32_ref[...].astype(BF16)
    wb_ref[...] = wb32_ref[...].astype(BF16)
    wo_ref[...] = wo32_ref[...].astype(BF16)
    _ffn_chunk(h_ref, o_ref, wa_ref[...], wb_ref[...], wo_ref[...])

    @pl.when(j == n_ff - 1)
    def _():
        _gated_residual_rows(x_ref, o_ref, mod_ref, g_ref, rstd_ref, sub, 0.5)


def _ffn_body(x_ref, mod_ref, g_ref, wa_ref, wb_ref, wo_ref, first_ref, o_ref, h_ref, rstd_ref, *,
              sub, n_ff, n_first):
    i = pl.program_id(0)
    j = pl.program_id(1)

    @pl.when(jnp.logical_and(i < n_first, j == 0))
    def _():
        o_ref[...] = first_ref[...]

    @pl.when(i >= n_first)
    def _():
        @pl.when(j == 0)
        def _():
            _modulate_rows(x_ref, mod_ref, g_ref, h_ref, sub)
            o_ref[...] = jnp.zeros_like(o_ref)

        _ffn_chunk(h_ref, o_ref, wa_ref[...], wb_ref[...], wo_ref[...])

        @pl.when(j == n_ff - 1)
        def _():
            _gated_residual_rows(x_ref, o_ref, mod_ref, g_ref, rstd_ref, sub, 0.5)


def _ffn_sublayer(cfg, t, mods, g6, w_in, w_out, layer, which, sub, n_rows, tm=512, fc=512, fc_first=256):
    d = t.shape[1]
    tm = _tile(cfg, tm)
    f = w_out.shape[2]
    row = _mod_row(cfg, tm)

    tm_first = FIRST_TILES * tm
    assert tm_first <= cfg.seq
    n1 = f // fc_first
    resident = pl.BlockSpec(memory_space=pltpu.VMEM)
    first, wa, wb, wo = pl.pallas_call(
        functools.partial(_ffn_first_body, sub=sub, n_ff=n1),
        grid=(n1,),
        in_specs=[
            resident,
            pl.BlockSpec((None, 9, d), lambda j: (0, 0, 0)),
            pl.BlockSpec((6, d), lambda j: (0, 0)),
            pl.BlockSpec((None, None, d, fc_first), lambda j: (layer, which, 0, j)),
            pl.BlockSpec((None, None, d, fc_first), lambda j: (layer, which, 0, n1 + j)),
            pl.BlockSpec((None, None, fc_first, d), lambda j: (layer, which, j, 0)),
        ],
        out_specs=[resident,
                   pl.BlockSpec((d, fc_first), lambda j: (0, j)),
                   pl.BlockSpec((d, fc_first), lambda j: (0, j)),
                   pl.BlockSpec((fc_first, d), lambda j: (j, 0))],
        out_shape=[jax.ShapeDtypeStruct((tm_first, d), F32), jax.ShapeDtypeStruct((d, f), BF16),
                   jax.ShapeDtypeStruct((d, f), BF16), jax.ShapeDtypeStruct((f, d), BF16)],
        scratch_shapes=[pltpu.VMEM((tm_first, d), BF16), pltpu.VMEM((tm_first, 1), F32)],
        compiler_params=_params("arbitrary"),
        name="ffn_first_tile",
    )(t[:tm_first], mods, g6, w_in, w_in, w_out)

    n_ff = f // fc
    chunk = lambda i, j: jnp.where(i < FIRST_TILES, 0, j)
    return pl.pallas_call(
        functools.partial(_ffn_body, sub=sub, n_ff=n_ff, n_first=FIRST_TILES),
        grid=(n_rows // tm, n_ff),
        in_specs=[
            pl.BlockSpec((tm, d), lambda i, j: (i, 0)),
            pl.BlockSpec((None, 9, d), lambda i, j: (row(i), 0, 0)),
            pl.BlockSpec((6, d), lambda i, j: (0, 0)),
            pl.BlockSpec((d, fc), lambda i, j: (0, chunk(i, j))),
            pl.BlockSpec((d, fc), lambda i, j: (0, chunk(i, j))),
            pl.BlockSpec((fc, d), lambda i, j: (chunk(i, j), 0)),
            pl.BlockSpec((tm, d), lambda i, j: (jnp.minimum(i, FIRST_TILES - 1), 0)),
        ],
        out_specs=pl.BlockSpec((tm, d), lambda i, j: (i, 0)),
        out_shape=jax.ShapeDtypeStruct((n_rows, d), F32),
        scratch_shapes=[pltpu.VMEM((tm, d), BF16), pltpu.VMEM((tm, 1), F32)],
        compiler_params=_params("arbitrary", "arbitrary"),
        name="ffn_sublayer",
    )(t, mods, g6, wa, wb, wo, first)


def _inproj_prologue(x_ref, mod_ref, g_ref, h_ref):
    @pl.when(pl.program_id(1) == 0)
    def _():
        _modulate_rows(x_ref, mod_ref, g_ref, h_ref, 1)


def _inproj_ret_body(x_ref, mod_ref, g_ref, cos_ref, sin_ref, w_ref, o_ref, kt_ref, h_ref, *, n_q, n_k):
    _inproj_prologue(x_ref, mod_ref, g_ref, h_ref)
    j = pl.program_id(1)
    tm, tn = o_ref.shape
    h = h_ref[...]

    def rotated_head(hd, scale):
        a = _dot(h, w_ref[:, hd * RET_DK:(hd + 1) * RET_DK])
        swapped = jnp.concatenate(
            [pltpu.roll(a[:, g * LANES:(g + 1) * LANES], LANES // 2, 1) for g in range(RET_DK // LANES)], axis=1)
        return (a * cos_ref[...] + swapped * sin_ref[...]) * scale

    @pl.when(j < n_q)
    def _():
        for hd in range(tn // RET_DK):
            o_ref[:, hd * RET_DK:(hd + 1) * RET_DK] = rotated_head(hd, 1.0).astype(BF16)

    @pl.when(jnp.logical_and(j >= n_q, j < n_q + n_k))
    def _():
        for hd in range(tn // RET_DK):
            r = rotated_head(hd, RET_DK ** -0.5)
            for ch in range(tm // CHUNK):
                kt_ref[ch, hd * RET_DK:(hd + 1) * RET_DK, :] = r[ch * CHUNK:(ch + 1) * CHUNK, :].T.astype(BF16)

    @pl.when(j >= n_q + n_k)
    def _():
        for hd in range(tn // RET_DK):
            cols = slice(hd * RET_DK, (hd + 1) * RET_DK)
            o_ref[:, cols] = _dot(h, w_ref[:, cols]).astype(BF16)


def _inproj_ret(cfg, t, mods, g6, w_all, inst, rope, tm=1024, tn=1024):
    n_tok, d = t.shape
    n_proj = w_all.shape[2]
    tm = _tile(cfg, tm)
    row = _mod_row(cfg, tm)
    lat_tiles = cfg.n_lat // tm
    per_seq = cfg.seq // tm
    trow = lambda i: jnp.where(i < lat_tiles, i % per_seq, per_seq + i - lat_tiles)
    n_q = RET_QK // tn
    n_k = RET_QK // tn
    pcol = lambda j: jnp.where(j < n_q, j, jnp.maximum(j - n_k, n_q - 1))
    kcol = lambda j: jnp.clip(j - n_q, 0, n_k - 1)
    return pl.pallas_call(
        functools.partial(_inproj_ret_body, n_q=n_q, n_k=n_k),
        grid=(n_tok // tm, n_proj // tn),
        in_specs=[
            pl.BlockSpec((tm, d), lambda i, j: (i, 0)),
            pl.BlockSpec((None, 9, d), lambda i, j: (row(i), 0, 0)),
            pl.BlockSpec((6, d), lambda i, j: (0, 0)),
            pl.BlockSpec((tm, RET_DK), lambda i, j: (trow(i), 0)),
            pl.BlockSpec((tm, RET_DK), lambda i, j: (trow(i), 0)),
            pl.BlockSpec((None, d, tn), lambda i, j: (inst, 0, j)),
        ],
        out_specs=[pl.BlockSpec((tm, tn), lambda i, j: (i, pcol(j))),
                   pl.BlockSpec((tm // CHUNK, tn, CHUNK), lambda i, j: (i, kcol(j), 0))],
        out_shape=[jax.ShapeDtypeStruct((n_tok, n_proj - RET_QK), BF16),
                   jax.ShapeDtypeStruct((n_tok // CHUNK, RET_QK, CHUNK), BF16)],
        scratch_shapes=[pltpu.VMEM((tm, d), BF16)],
        compiler_params=_params("arbitrary", "arbitrary"),
        name="inproj_ret",
    )(t, mods, g6, rope[0], rope[1], w_all)


INPROJ_PIECE = 256


def _inproj_gelu_body(x_ref, mod_ref, g_ref, w_ref, o_ref, h_ref):
    _inproj_prologue(x_ref, mod_ref, g_ref, h_ref)
    h = h_ref[...]
    for s in range(o_ref.shape[1] // INPROJ_PIECE):
        cols = slice(s * INPROJ_PIECE, (s + 1) * INPROJ_PIECE)
        o_ref[:, cols] = _gelu_tanh(_wdot(h, w_ref, cols)).astype(BF16)


def _inproj_glu_body(x_ref, mod_ref, g_ref, wa_ref, wb_ref, o_ref, h_ref):
    _inproj_prologue(x_ref, mod_ref, g_ref, h_ref)
    h = h_ref[...]
    for s in range(o_ref.shape[1] // INPROJ_PIECE):
        cols = slice(s * INPROJ_PIECE, (s + 1) * INPROJ_PIECE)
        o_ref[:, cols] = (_dot(h, wa_ref[:, cols]) * jax.nn.sigmoid(_dot(h, wb_ref[:, cols]))).astype(BF16)


def _inproj(cfg, kind, t, mods, g6, w, tm=1024, tn=1024):
    n_tok, d = t.shape
    tm = _tile(cfg, tm)
    row = _mod_row(cfg, tm)
    common = [
        pl.BlockSpec((tm, d), lambda i, j: (i, 0)),
        pl.BlockSpec((None, 9, d), lambda i, j: (row(i), 0, 0)),
        pl.BlockSpec((6, d), lambda i, j: (0, 0)),
    ]
    if kind == "gelu":
        n_out = w.shape[1]
        body = _inproj_gelu_body
        specs = common + [pl.BlockSpec((d, tn), lambda i, j: (0, j))]
        args = (t, mods, g6, w)
    else:
        n_out = w.shape[1] // 2
        nb = n_out // tn
        body = _inproj_glu_body
        specs = common + [pl.BlockSpec((d, tn), lambda i, j: (0, j)),
                          pl.BlockSpec((d, tn), lambda i, j: (0, nb + j))]
        args = (t, mods, g6, w, w)
    return pl.pallas_call(
        body,
        grid=(n_tok // tm, n_out // tn),
        in_specs=specs,
        out_specs=pl.BlockSpec((tm, tn), lambda i, j: (i, j)),
        out_shape=jax.ShapeDtypeStruct((n_tok, n_out), BF16),
        scratch_shapes=[pltpu.VMEM((tm, d), BF16)],
        compiler_params=_params("arbitrary", "arbitrary"),
        name="inproj_" + kind,
    )(*args)


SCAN_CHUNK = 2 * CHUNK


def _ret_body(dec_ref, gn_ref, ql_ref, ktl_ref, vl_ref, gl_ref, qc_ref, ktc_ref, vc_ref, gc_ref, ol_ref, oc_ref,
              sf_ref, sb_ref, af_ref, ab_ref, *, n_lat_chunks, n_ctx_chunks):
    c = SCAN_CHUNK
    sub = c // CHUNK
    lgf = dec_ref[0:1, 0:1]
    lgb = dec_ref[1:2, 0:1]
    diff = (lax.broadcasted_iota(jnp.int32, (c, c), 0) - lax.broadcasted_iota(jnp.int32, (c, c), 1)).astype(F32)
    dmat = jnp.where(diff >= 0.0, jnp.exp(jnp.maximum(diff, 0.0) * lgf), jnp.exp(jnp.maximum(-diff, 0.0) * lgb))
    pos = lax.broadcasted_iota(jnp.int32, (c, 1), 0).astype(F32)
    lane = lax.broadcasted_iota(jnp.int32, (1, c), 1).astype(F32)
    qdec_f = jnp.exp((pos + 1.0) * lgf)
    kdec_f = jnp.exp((c - 1.0 - lane) * lgf)
    cdec_f = jnp.exp(c * lgf)
    qdec_b = jnp.exp((c - pos) * lgb)
    kdec_b = jnp.exp(lane * lgb)
    cdec_b = jnp.exp(c * lgb)

    def kt_chunk(kt_ref, i):
        return jnp.concatenate([kt_ref[i * sub + s] for s in range(sub)], axis=1)

    def pair(q_ref, kt_ref, v_ref, cf, rf, cb, rb):
        q = q_ref[rf, :]
        kt = kt_chunk(kt_ref, cf)
        v = v_ref[rf, :]
        a = _dot(q, kt) * dmat
        af_ref[rf, :] = _dot(a.astype(BF16), v) + _dot(q, sf_ref[...].astype(BF16)) * qdec_f
        sf_ref[...] = sf_ref[...] * cdec_f + _dot((kt.astype(F32) * kdec_f).astype(BF16), v)
        q = q_ref[rb, :]
        kt = kt_chunk(kt_ref, cb)
        v = v_ref[rb, :]
        ab_ref[rb, :] = _dot(q, sb_ref[...].astype(BF16)) * qdec_b
        sb_ref[...] = sb_ref[...] * cdec_b + _dot((kt.astype(F32) * kdec_b).astype(BF16), v)

    def finish(g_ref, o_ref, rows):
        o = af_ref[rows, :] + ab_ref[rows, :]
        o_ref[rows, :] = (_silu(g_ref[rows, :].astype(F32)) * (_standardize(o) * gn_ref[...])).astype(BF16)

    def chunk_rows(i):
        return pl.ds(pl.multiple_of(i * c, c), c)

    sf_ref[...] = jnp.zeros_like(sf_ref)
    sb_ref[...] = jnp.zeros_like(sb_ref)
    for i in range(n_ctx_chunks):
        ib = n_ctx_chunks - 1 - i
        pair(qc_ref, ktc_ref, vc_ref, i, slice(i * c, (i + 1) * c), ib, slice(ib * c, (ib + 1) * c))
    for i in range(n_ctx_chunks):
        finish(gc_ref, oc_ref, slice(i * c, (i + 1) * c))

    def scan_step(i, carry):
        ib = n_lat_chunks - 1 - i
        pair(ql_ref, ktl_ref, vl_ref, i, chunk_rows(i), ib, chunk_rows(ib))
        return carry

    lax.fori_loop(0, n_lat_chunks, scan_step, 0)

    def finish_step(i, carry):
        finish(gl_ref, ol_ref, chunk_rows(i))
        return carry

    lax.fori_loop(0, n_lat_chunks, finish_step, 0)


def _ret_scan(cfg, p, kt, dec, gn_g):
    assert cfg.seq % SCAN_CHUNK == 0 and cfg.ctx % SCAN_CHUNK == 0
    v_off = RET_QK // RET_DV
    g_off = v_off + RET_V // RET_DV
    ctx0 = cfg.n_lat // cfg.ctx
    return pl.pallas_call(
        functools.partial(_ret_body, n_lat_chunks=cfg.seq // SCAN_CHUNK, n_ctx_chunks=cfg.ctx // SCAN_CHUNK),
        grid=(cfg.batch, RET_HEADS),
        in_specs=[
            pl.BlockSpec((None, 8, LANES), lambda b, h: (h, 0, 0)),
            pl.BlockSpec((1, RET_DV), lambda b, h: (0, h)),
            pl.BlockSpec((cfg.seq, RET_DK), lambda b, h: (b, h)),
            pl.BlockSpec((cfg.seq // CHUNK, RET_DK, CHUNK), lambda b, h: (b, h, 0)),
            pl.BlockSpec((cfg.seq, RET_DV), lambda b, h: (b, v_off + h)),
            pl.BlockSpec((cfg.seq, RET_DV), lambda b, h: (b, g_off + h)),
            pl.BlockSpec((cfg.ctx, RET_DK), lambda b, h: (ctx0 + b, h)),
            pl.BlockSpec((cfg.ctx // CHUNK, RET_DK, CHUNK), lambda b, h: (ctx0 + b, h, 0)),
            pl.BlockSpec((cfg.ctx, RET_DV), lambda b, h: (ctx0 + b, v_off + h)),
            pl.BlockSpec((cfg.ctx, RET_DV), lambda b, h: (ctx0 + b, g_off + h)),
        ],
        out_specs=[pl.BlockSpec((cfg.seq, RET_DV), lambda b, h: (b, h)),
                   pl.BlockSpec((cfg.ctx, RET_DV), lambda b, h: (b, h))],
        out_shape=[jax.ShapeDtypeStruct((cfg.n_lat, RET_V), BF16),
                   jax.ShapeDtypeStruct((cfg.n_ctx, RET_V), BF16)],
        scratch_shapes=[pltpu.VMEM((RET_DK, RET_DV), F32), pltpu.VMEM((RET_DK, RET_DV), F32),
                        pltpu.VMEM((cfg.seq, RET_DV), F32), pltpu.VMEM((cfg.seq, RET_DV), F32)],
        compiler_params=_params("arbitrary", "arbitrary"),
        name="ret_scan",
    )(dec, gn_g, p, kt, p, p, p, kt, p, p)


def _outproj_body(al_ref, ac_ref, w_ref, x_ref, mod_ref, g_ref, o_ref, rstd_ref, *, n_k, lat_tiles):
    i = pl.program_id(0)
    kk = pl.program_id(1)

    @pl.when(kk == 0)
    def _():
        o_ref[...] = jnp.zeros_like(o_ref)

    @pl.when(i < lat_tiles)
    def _():
        o_ref[...] += _dot(al_ref[...], w_ref[...])

    @pl.when(i >= lat_tiles)
    def _():
        o_ref[...] += _dot(ac_ref[...], w_ref[...])

    @pl.when(kk == n_k - 1)
    def _():
        _gated_residual_rows(x_ref, o_ref, mod_ref, g_ref, rstd_ref, 1, 1.0)


def _outproj(cfg, a_lat, a_ctx, w_all, inst, t, mods, g6, n_rows, tm=1024, kc=1024):
    d = t.shape[1]
    tm = _tile(cfg, tm)
    n_k = a_lat.shape[1] // kc
    row = _mod_row(cfg, tm)
    lat_tiles = cfg.n_lat // tm
    lat_idx = lambda i, k: (jnp.minimum(i, lat_tiles - 1), jnp.where(i < lat_tiles, k, n_k - 1))
    ctx_idx = lambda i, k: (jnp.maximum(i - lat_tiles, 0), jnp.where(i < lat_tiles, 0, k))
    return pl.pallas_call(
        functools.partial(_outproj_body, n_k=n_k, lat_tiles=lat_tiles),
        grid=(n_rows // tm, n_k),
        in_specs=[
            pl.BlockSpec((tm, kc), lat_idx),
            pl.BlockSpec((tm, kc), ctx_idx),
            pl.BlockSpec((None, kc, d), lambda i, k: (inst, k, 0)),
            pl.BlockSpec((tm, d), lambda i, k: (i, 0)),
            pl.BlockSpec((None, 9, d), lambda i, k: (row(i), 0, 0)),
            pl.BlockSpec((6, d), lambda i, k: (0, 0)),
        ],
        out_specs=pl.BlockSpec((tm, d), lambda i, k: (i, 0)),
        out_shape=jax.ShapeDtypeStruct((n_rows, d), F32),
        scratch_shapes=[pltpu.VMEM((tm, 1), F32)],
        compiler_params=_params("arbitrary", "arbitrary"),
        name="outproj",
    )(a_lat, a_ctx, w_all, t, mods, g6)


def _gmlp_body(vfull_ref, v_ref, u_ref, lng_ref, lnb_ref, ws_ref, bs_ref, w_ref, x_ref, mod_ref, g_ref,
               o_ref, mean_ref, rstd_ref, gated_ref, *, n_groups):
    grp = pl.program_id(1)
    tm = v_ref.shape[0]

    @pl.when(grp == 0)
    def _():
        v = vfull_ref[...].astype(F32)
        mu = jnp.mean(v, axis=-1, keepdims=True)
        vc = v - mu
        mean_ref[...] = mu
        rstd_ref[...] = lax.rsqrt(jnp.mean(vc * vc, axis=-1, keepdims=True) + EPS)
        o_ref[...] = jnp.zeros_like(o_ref)

    vn = ((v_ref[...].astype(F32) - mean_ref[...]) * rstd_ref[...] * lng_ref[...] + lnb_ref[...]).astype(BF16)
    ws = ws_ref[...]
    bias = bs_ref[:, 0:1]
    for i in range(tm // CHUNK):
        sl = slice(i * CHUNK, (i + 1) * CHUNK)
        mixed = _dot(ws, vn[sl, :]) + bias
        gated_ref[sl, :] = (u_ref[sl, :].astype(F32) * mixed).astype(BF16)
    o_ref[...] += _dot(gated_ref[...], w_ref[...])

    @pl.when(grp == n_groups - 1)
    def _():
        _gated_residual_rows(x_ref, o_ref, mod_ref, g_ref, rstd_ref, 1, 1.0)


def _gmlp_mix(cfg, z, ln_g, ln_b, w_s, b_s, w_out, t, mods, g6, tm=512):
    n_tok, d = t.shape
    tm = _tile(cfg, tm)
    ge = GMLP_GE
    row = _mod_row(cfg, tm)
    return pl.pallas_call(
        functools.partial(_gmlp_body, n_groups=GMLP_GROUPS),
        grid=(n_tok // tm, GMLP_GROUPS),
        in_specs=[
            pl.BlockSpec((tm, GMLP_E), lambda i, g: (i, 1)),
            pl.BlockSpec((tm, ge), lambda i, g: (i, GMLP_GROUPS + g)),
            pl.BlockSpec((tm, ge), lambda i, g: (i, g)),
            pl.BlockSpec((1, ge), lambda i, g: (0, g)),
            pl.BlockSpec((1, ge), lambda i, g: (0, g)),
            pl.BlockSpec((None, CHUNK, CHUNK), lambda i, g: (g, 0, 0)),
            pl.BlockSpec((None, CHUNK, LANES), lambda i, g: (g, 0, 0)),
            pl.BlockSpec((ge, d), lambda i, g: (g, 0)),
            pl.BlockSpec((tm, d), lambda i, g: (i, 0)),
            pl.BlockSpec((None, 9, d), lambda i, g: (row(i), 0, 0)),
            pl.BlockSpec((6, d), lambda i, g: (0, 0)),
        ],
        out_specs=pl.BlockSpec((tm, d), lambda i, g: (i, 0)),
        out_shape=jax.ShapeDtypeStruct((n_tok, d), F32),
        scratch_shapes=[pltpu.VMEM((tm, 1), F32), pltpu.VMEM((tm, 1), F32), pltpu.VMEM((tm, ge), BF16)],
        compiler_params=_params("arbitrary", "arbitrary"),
        name="gmlp_mix",
    )(z, z, z, ln_g, ln_b, w_s, b_s, w_out, t, mods, g6)


CONV_TM = 256


def _conv_body(prev_ref, cur_ref, next_ref, wdw_ref, bdw_ref, lng_ref, lnb_ref, w_ref, x_ref, mod_ref, g_ref,
               o_ref, zs_ref, cz_ref, h_ref, rstd_ref, *, lat_tiles, per_seq):
    i = pl.program_id(0)
    tm = cur_ref.shape[0]
    n_strips = cur_ref.shape[1] // LANES
    hal = CONV_HALO
    pad = CONV_K // 2
    is_ctx = i >= lat_tiles
    first = jnp.logical_or(is_ctx, i % per_seq == 0)
    last = jnp.logical_or(is_ctx, i % per_seq == per_seq - 1)
    pscale = jnp.where(first, 0.0, 1.0)
    nscale = jnp.where(last, 0.0, 1.0)
    for s in range(n_strips):
        ls = slice(s * LANES, (s + 1) * LANES)
        zs_ref[s, 0:hal, :] = prev_ref[:, ls].astype(F32) * pscale
        zs_ref[s, hal:hal + tm, :] = cur_ref[:, ls].astype(F32)
        zs_ref[s, hal + tm:hal + tm + hal, :] = next_ref[:, ls].astype(F32) * nscale

    def strip(s, carry):
        acc = jnp.zeros((tm, LANES), F32) + bdw_ref[s]
        for k in range(CONV_K):
            acc = acc + wdw_ref[s, k:k + 1, :] * zs_ref[s, hal - pad + k:hal - pad + k + tm, :]
        cz_ref[s] = acc
        return carry

    lax.fori_loop(0, n_strips, strip, 0)

    tot = jnp.zeros((tm, 1), F32)
    for s in range(n_strips):
        tot = tot + jnp.sum(cz_ref[s], axis=-1, keepdims=True)
    mu = tot / (n_strips * LANES)
    sq = jnp.zeros((tm, 1), F32)
    for s in range(n_strips):
        dlt = cz_ref[s] - mu
        sq = sq + jnp.sum(dlt * dlt, axis=-1, keepdims=True)
    rstd = lax.rsqrt(sq / (n_strips * LANES) + EPS)
    for s in range(n_strips):
        ls = slice(s * LANES, (s + 1) * LANES)
        y = (cz_ref[s] - mu) * rstd * lng_ref[:, ls] + lnb_ref[:, ls]
        h_ref[:, ls] = _silu(y).astype(BF16)
    o_ref[...] = _dot(h_ref[...], w_ref[...])
    _gated_residual_rows(x_ref, o_ref, mod_ref, g_ref, rstd_ref, 1, 1.0)


def _conv_mix(cfg, z, w_dw, b_dw, ln_g, ln_b, w_pw2, t, mods, g6):
    n_tok, d = t.shape
    tm = CONV_TM
    assert cfg.ctx == tm
    hal = CONV_HALO
    n_strips = d // LANES
    hb = tm // hal
    n_hal = n_tok // hal
    row = _mod_row(cfg, tm)
    return pl.pallas_call(
        functools.partial(_conv_body, lat_tiles=cfg.n_lat // tm, per_seq=cfg.seq // tm),
        grid=(n_tok // tm,),
        in_specs=[
            pl.BlockSpec((hal, d), lambda i: (jnp.maximum(i * hb - 1, 0), 0)),
            pl.BlockSpec((tm, d), lambda i: (i, 0)),
            pl.BlockSpec((hal, d), lambda i: (jnp.minimum((i + 1) * hb, n_hal - 1), 0)),
            pl.BlockSpec((n_strips, 32, LANES), lambda i: (0, 0, 0)),
            pl.BlockSpec((n_strips, 1, LANES), lambda i: (0, 0, 0)),
            pl.BlockSpec((1, d), lambda i: (0, 0)),
            pl.BlockSpec((1, d), lambda i: (0, 0)),
            pl.BlockSpec((d, d), lambda i: (0, 0)),
            pl.BlockSpec((tm, d), lambda i: (i, 0)),
            pl.BlockSpec((None, 9, d), lambda i: (row(i), 0, 0)),
            pl.BlockSpec((6, d), lambda i: (0, 0)),
        ],
        out_specs=pl.BlockSpec((tm, d), lambda i: (i, 0)),
        out_shape=jax.ShapeDtypeStruct((n_tok, d), F32),
        scratch_shapes=[pltpu.VMEM((n_strips, tm + 2 * hal, LANES), F32),
                        pltpu.VMEM((n_strips, tm, LANES), F32),
                        pltpu.VMEM((tm, d), BF16), pltpu.VMEM((tm, 1), F32)],
        compiler_params=_params("arbitrary"),
        name="conv_mix",
    )(z, z, z, w_dw, b_dw, ln_g, ln_b, w_pw2, t, mods, g6)


def _rope_tables(cfg, n_ident):
    quarter = RET_DK // 4
    pos = jnp.arange(cfg.seq, dtype=jnp.int32)
    rows = (pos // GRID_W).astype(F32)
    cols = (pos % GRID_W).astype(F32)
    inv = ROPE_BASE ** (-jnp.arange(quarter, dtype=F32) / quarter)
    ar = rows[:, None] * inv[None, :]
    ac = cols[:, None] * inv[None, :]
    cos = jnp.concatenate([jnp.cos(ar), jnp.cos(ar), jnp.cos(ac), jnp.cos(ac)], axis=1)
    sin = jnp.concatenate([-jnp.sin(ar), jnp.sin(ar), -jnp.sin(ac), jnp.sin(ac)], axis=1)
    cos = jnp.concatenate([cos, jnp.ones((n_ident, RET_DK), F32)], axis=0)
    sin = jnp.concatenate([sin, jnp.zeros((n_ident, RET_DK), F32)], axis=0)
    return cos, sin


def _forward(cfg, x, c, ctx, c_ctx, ada_w, ada_b, norm_g, ffn_w_in, ffn_w_out, ret_w_in, ret_w_out,
             ret_decay_logit, ret_gn_g, gmlp_w_in, gmlp_ln_g, gmlp_ln_b, gmlp_w_s, gmlp_b_s, gmlp_w_out,
             conv_w_pw1, conv_w_dw, conv_b_dw, conv_ln_g, conv_ln_b, conv_w_pw2):
    d = x.shape[-1]
    depth = ada_w.shape[0]
    t = jnp.concatenate([x.reshape(cfg.n_lat, d), ctx.reshape(cfg.n_ctx, d)], axis=0)
    cc = jnp.zeros((ADA_ROWS, d), F32).at[:cfg.batch].set(c).at[cfg.batch].set(c_ctx)
    mods_all = _ada_table(cc, ada_w, ada_b)
    ret_w_in_bf = ret_w_in.astype(BF16)
    ret_w_out_bf = ret_w_out.astype(BF16)
    rope = _rope_tables(cfg, cfg.n_ctx)

    for i in range(depth):
        kind = i % N_MIXERS
        inst = i // N_MIXERS
        last = i == depth - 1
        mods = mods_all[i]
        g6 = norm_g[i].reshape(6, d)
        n_after = cfg.n_lat if last else cfg.n_tok
        t = _ffn_sublayer(cfg, t, mods, g6, ffn_w_in, ffn_w_out, i, 0, 0, cfg.n_tok)
        if kind == 0:
            p, kt = _inproj_ret(cfg, t, mods, g6, ret_w_in_bf, inst, rope)
            log_g = jax.nn.log_sigmoid(ret_decay_logit[inst].astype(F32))
            dec = jnp.zeros((RET_HEADS, 8, LANES), F32).at[:, 0:2, :].set(
                jnp.broadcast_to(log_g.T[:, :, None], (RET_HEADS, 2, LANES)))
            a_lat, a_ctx = _ret_scan(cfg, p, kt, dec, ret_gn_g[inst].reshape(1, RET_V))
            t = _outproj(cfg, a_lat, a_ctx, ret_w_out_bf, inst, t, mods, g6, n_after)
        elif kind == 1:
            z = _inproj(cfg, "gelu", t, mods, g6, gmlp_w_in[inst])
            b_s = jnp.broadcast_to(gmlp_b_s[inst][:, :, None], (GMLP_GROUPS, CHUNK, LANES))
            t = _gmlp_mix(cfg, z, gmlp_ln_g[inst].reshape(1, GMLP_E), gmlp_ln_b[inst].reshape(1, GMLP_E),
                          gmlp_w_s[inst].astype(BF16), b_s, gmlp_w_out[inst].astype(BF16), t, mods, g6)
        else:
            z = _inproj(cfg, "glu", t, mods, g6, conv_w_pw1[inst].astype(BF16))
            n_strips = d // LANES
            w_dw = jnp.zeros((32, d), F32).at[:CONV_K].set(conv_w_dw[inst])
            w_dw = w_dw.reshape(32, n_strips, LANES).transpose(1, 0, 2)
            t = _conv_mix(cfg, z, w_dw, conv_b_dw[inst].reshape(n_strips, 1, LANES),
                          conv_ln_g[inst].reshape(1, d), conv_ln_b[inst].reshape(1, d),
                          conv_w_pw2[inst].astype(BF16), t, mods, g6)
        t = _ffn_sublayer(cfg, t, mods, g6, ffn_w_in, ffn_w_out, i, 1, 2, n_after)
    return t[:cfg.n_lat].reshape(x.shape)


def kernel(x, c, ctx, c_ctx, ada_w, ada_b, norm_g, ffn_w_in, ffn_w_out, ret_w_in, ret_w_out, ret_decay_logit,
           ret_gn_g, gmlp_w_in, gmlp_ln_g, gmlp_ln_b, gmlp_w_s, gmlp_b_s, gmlp_w_out, conv_w_pw1, conv_w_dw,
           conv_b_dw, conv_ln_g, conv_ln_b, conv_w_pw2):
    cfg = _Cfg(batch=x.shape[0], seq=x.shape[1], ctx=ctx.shape[1])
    return _forward(cfg, x, c, ctx, c_ctx, ada_w, ada_b, norm_g, ffn_w_in, ffn_w_out, ret_w_in, ret_w_out,
                    ret_decay_logit, ret_gn_g, gmlp_w_in, gmlp_ln_g, gmlp_ln_b, gmlp_w_s, gmlp_b_s, gmlp_w_out,
                    conv_w_pw1, conv_w_dw, conv_b_dw, conv_ln_g, conv_ln_b, conv_w_pw2)
```

```python
import functools
from typing import NamedTuple

import jax
import jax.numpy as jnp
from jax import lax
from jax.experimental import pallas as pl
from jax.experimental.pallas import tpu as pltpu

F32 = jnp.float32
BF16 = jnp.bfloat16

D_MODEL = 2048
DEPTH = 4
GRID_W = 64
N_MIXERS = 3
D_FF = 5632
RET_HEADS = 8
RET_DK = D_MODEL // RET_HEADS
RET_DV = 2 * RET_DK
RET_QK = RET_HEADS * RET_DK
RET_V = RET_HEADS * RET_DV
RET_IN = 2 * RET_QK + 2 * RET_V
CHUNK = 128
ROPE_BASE = 10000.0
GMLP_GROUPS = 8
GMLP_E = 3 * D_MODEL
GMLP_GE = GMLP_E // GMLP_GROUPS
CONV_K = 31
CONV_HALO = 16
EPS = 1e-6
ADA_ROWS = 8
LANES = 128

V7X_VMEM_BYTES = 64 * 1024 * 1024
VMEM_LIMIT = V7X_VMEM_BYTES - 4 * 1024 * 1024


class _Cfg(NamedTuple):
    batch: int
    seq: int
    ctx: int

    @property
    def n_lat(self):
        return self.batch * self.seq

    @property
    def n_ctx(self):
        return self.batch * self.ctx

    @property
    def n_tok(self):
        return self.n_lat + self.n_ctx


def _params(*sem):
    return pltpu.CompilerParams(dimension_semantics=sem, vmem_limit_bytes=VMEM_LIMIT)


def _tile(cfg, want):
    tm = want
    while cfg.seq % tm or cfg.n_ctx % tm:
        tm //= 2
    return tm


def _mod_row(cfg, tm):
    lat_tiles = cfg.n_lat // tm
    per_seq = cfg.seq // tm
    return lambda i: jnp.where(i < lat_tiles, i // per_seq, cfg.batch)


NORM_ROWS = 16


def _modulate_rows(x_ref, mod_ref, g_ref, h_ref, sub):
    x = x_ref[...]
    xn = x * lax.rsqrt(jnp.mean(x * x, axis=-1, keepdims=True) + EPS)
    h = (xn * g_ref[2 * sub:2 * sub + 1, :]) * (1.0 + mod_ref[3 * sub + 1:3 * sub + 2, :])
    h_ref[...] = (h + mod_ref[3 * sub:3 * sub + 1, :]).astype(BF16)


def _gated_residual_rows(x_ref, o_ref, mod_ref, g_ref, rstd_ref, sub, weight):
    blocks = [slice(i, i + NORM_ROWS) for i in range(0, x_ref.shape[0], NORM_ROWS)]
    for rows in blocks:
        v = o_ref[rows, :]
        rstd_ref[rows, :] = lax.rsqrt(jnp.mean(v * v, axis=-1, keepdims=True) + EPS)
    for rows in blocks:
        yn = o_ref[rows, :] * rstd_ref[rows, :]
        gate = weight * mod_ref[3 * sub + 2:3 * sub + 3, :]
        o_ref[rows, :] = x_ref[rows, :] + gate * (yn * g_ref[2 * sub + 1:2 * sub + 2, :])


def _standardize(x):
    xc = x - jnp.mean(x, axis=-1, keepdims=True)
    return xc * lax.rsqrt(jnp.mean(xc * xc, axis=-1, keepdims=True) + EPS)


def _silu(x):
    return x * jax.nn.sigmoid(x)


def _gelu_tanh(x):
    return 0.5 * x * (1.0 + jnp.tanh(0.7978845608028654 * (x + 0.044715 * (x * x * x))))


def _dot(a, b):
    return jnp.dot(a, b, preferred_element_type=F32)


def _wdot(a, w_ref, cols=slice(None)):
    return _dot(a, w_ref[:, cols].astype(BF16))


def _ada_body(c_ref, w_ref, b_ref, o_ref):
    s = _silu(c_ref[...]).astype(BF16)
    o_ref[...] = _dot(s, w_ref[...].astype(BF16)) + b_ref[...]


def _ada_table(cc, ada_w, ada_b):
    depth, d, n = ada_w.shape
    tn = 1024
    out = pl.pallas_call(
        _ada_body,
        grid=(depth, n // tn),
        in_specs=[
            pl.BlockSpec((ADA_ROWS, d), lambda i, j: (0, 0)),
            pl.BlockSpec((None, d, tn), lambda i, j: (i, 0, j)),
            pl.BlockSpec((None, 1, tn), lambda i, j: (i, 0, j)),
        ],
        out_specs=pl.BlockSpec((None, ADA_ROWS, tn), lambda i, j: (i, 0, j)),
        out_shape=jax.ShapeDtypeStruct((depth, ADA_ROWS, n), F32),
        compiler_params=_params("arbitrary", "arbitrary"),
        name="ada_table",
    )(cc, ada_w, ada_b.reshape(depth, 1, n))
    return out.reshape(depth, ADA_ROWS, n // d, d)


FIRST_TILES = 2


def _ffn_chunk(h_ref, wa, wb, wo):
    h = h_ref[...]
    act = (_silu(_dot(h, wa)) * _dot(h, wb)).astype(BF16)
    return _dot(act, wo)


def _ffn_first_body(x_ref, mod_ref, g_ref, wa32_ref, wb32_ref, wo32_ref, o_ref, wa_ref, wb_ref, wo_ref,
                    h_ref, rstd_ref, *, sub, n_ff):
    j = pl.program_id(0)

    @pl.when(j == 0)
    def _():
        _modulate_rows(x_ref, mod_ref, g_ref, h_ref, sub)
        o_ref[...] = jnp.zeros_like(o_ref)

    wa_ref[...] = wa32_ref[...].astype(BF16)
    wb_ref[...] = wb32_ref[...].astype(BF16)
    wo_ref[...] = wo32_ref[...].astype(BF16)
    o_ref[...] += _ffn_chunk(h_ref, wa_ref[...], wb_ref[...], wo_ref[...])

    @pl.when(j == n_ff - 1)
    def _():
        _gated_residual_rows(x_ref, o_ref, mod_ref, g_ref, rstd_ref, sub, 0.5)


def _ffn_body(x_ref, mod_ref, g_ref, wa_ref, wb_ref, wo_ref, first_ref, o_ref, h_ref, rstd_ref, *,
              sub, n_ff, n_first):
    i = pl.program_id(0)
    j = pl.program_id(1)

    @pl.when(jnp.logical_and(i < n_first, j == 0))
    def _():
        o_ref[...] = first_ref[...]

    @pl.when(i >= n_first)
    def _():
        @pl.when(j == 0)
        def _():
            _modulate_rows(x_ref, mod_ref, g_ref, h_ref, sub)
            o_ref[...] = _ffn_chunk(h_ref, wa_ref[...], wb_ref[...], wo_ref[...])

        @pl.when(jnp.logical_and(j > 0, j < n_ff - 1))
        def _():
            o_ref[...] += _ffn_chunk(h_ref, wa_ref[...], wb_ref[...], wo_ref[...])

        @pl.when(j == n_ff - 1)
        def _():
            o_ref[...] += _ffn_chunk(h_ref, wa_ref[...], wb_ref[...], wo_ref[...])
            _gated_residual_rows(x_ref, o_ref, mod_ref, g_ref, rstd_ref, sub, 0.5)


def _ffn_split_body(xl_ref, xc_ref, *refs, lat_tiles, **kw):
    i = pl.program_id(0)

    @pl.when(i < lat_tiles)
    def _():
        _ffn_body(xl_ref, *refs, **kw)

    @pl.when(i >= lat_tiles)
    def _():
        _ffn_body(xc_ref, *refs, **kw)


def _ffn_sublayer(cfg, t, mods, g6, w_in, w_out, layer, which, sub, n_rows, t_ctx=None,
                  tm=512, fc=512, fc_first=256):
    d = t.shape[1]
    tm = _tile(cfg, tm)
    f = w_out.shape[2]
    row = _mod_row(cfg, tm)

    tm_first = FIRST_TILES * tm
    assert tm_first <= cfg.seq
    n1 = f // fc_first
    resident = pl.BlockSpec(memory_space=pltpu.VMEM)
    first, wa, wb, wo = pl.pallas_call(
        functools.partial(_ffn_first_body, sub=sub, n_ff=n1),
        grid=(n1,),
        in_specs=[
            resident,
            pl.BlockSpec((None, 9, d), lambda j: (0, 0, 0)),
            pl.BlockSpec((6, d), lambda j: (0, 0)),
            pl.BlockSpec((None, None, d, fc_first), lambda j: (layer, which, 0, j)),
            pl.BlockSpec((None, None, d, fc_first), lambda j: (layer, which, 0, n1 + j)),
            pl.BlockSpec((None, None, fc_first, d), lambda j: (layer, which, j, 0)),
        ],
        out_specs=[resident,
                   pl.BlockSpec((d, fc_first), lambda j: (0, j)),
                   pl.BlockSpec((d, fc_first), lambda j: (0, j)),
                   pl.BlockSpec((fc_first, d), lambda j: (j, 0))],
        out_shape=[jax.ShapeDtypeStruct((tm_first, d), F32), jax.ShapeDtypeStruct((d, f), BF16),
                   jax.ShapeDtypeStruct((d, f), BF16), jax.ShapeDtypeStruct((f, d), BF16)],
        scratch_shapes=[pltpu.VMEM((tm_first, d), BF16), pltpu.VMEM((tm_first, 1), F32)],
        compiler_params=_params("arbitrary"),
        name="ffn_first_tile",
    )(t[:tm_first], mods, g6, w_in, w_in, w_out)

    n_ff = f // fc
    chunk = lambda i, j: jnp.where(i < FIRST_TILES, 0, j)
    kw = dict(sub=sub, n_ff=n_ff, n_first=FIRST_TILES)
    if t_ctx is None:
        body = functools.partial(_ffn_body, **kw)
        streams = (t,)
        x_specs = [pl.BlockSpec((tm, d), lambda i, j: (i, 0))]
    else:
        lat_tiles = cfg.n_lat // tm
        body = functools.partial(_ffn_split_body, lat_tiles=lat_tiles, **kw)
        streams = (t, t_ctx)
        x_specs = [pl.BlockSpec((tm, d), lambda i, j: (jnp.minimum(i, lat_tiles - 1), 0)),
                   pl.BlockSpec((tm, d), lambda i, j: (jnp.maximum(i - lat_tiles, 0), 0))]
    return pl.pallas_call(
        body,
        grid=(n_rows // tm, n_ff),
        in_specs=x_specs + [
            pl.BlockSpec((None, 9, d), lambda i, j: (row(i), 0, 0)),
            pl.BlockSpec((6, d), lambda i, j: (0, 0)),
            pl.BlockSpec((d, fc), lambda i, j: (0, chunk(i, j))),
            pl.BlockSpec((d, fc), lambda i, j: (0, chunk(i, j))),
            pl.BlockSpec((fc, d), lambda i, j: (chunk(i, j), 0)),
            pl.BlockSpec((tm, d), lambda i, j: (jnp.minimum(i, FIRST_TILES - 1), 0)),
        ],
        out_specs=pl.BlockSpec((tm, d), lambda i, j: (i, 0)),
        out_shape=jax.ShapeDtypeStruct((n_rows, d), F32),
        scratch_shapes=[pltpu.VMEM((tm, d), BF16), pltpu.VMEM((tm, 1), F32)],
        compiler_params=_params("arbitrary", "arbitrary"),
        name="ffn_sublayer",
    )(*streams, mods, g6, wa, wb, wo, first)


def _inproj_ret_body(x_ref, mod_ref, g_ref, cos_ref, sin_ref, w_ref, o_ref, kt_ref, h_ref, *, n_q, n_k):
    j = pl.program_id(1)
    tm, tn = o_ref.shape

    def rotated_head(h, hd, scale):
        a = _dot(h, w_ref[:, hd * RET_DK:(hd + 1) * RET_DK])
        swapped = jnp.concatenate(
            [pltpu.roll(a[:, g * LANES:(g + 1) * LANES], LANES // 2, 1) for g in range(RET_DK // LANES)], axis=1)
        return (a * cos_ref[...] + swapped * sin_ref[...]) * scale

    def q_tile():
        h = h_ref[...]
        for hd in range(tn // RET_DK):
            o_ref[:, hd * RET_DK:(hd + 1) * RET_DK] = rotated_head(h, hd, 1.0).astype(BF16)

    @pl.when(j == 0)
    def _():
        _modulate_rows(x_ref, mod_ref, g_ref, h_ref, 1)
        q_tile()

    @pl.when(jnp.logical_and(j > 0, j < n_q))
    def _():
        q_tile()

    @pl.when(jnp.logical_and(j >= n_q, j < n_q + n_k))
    def _():
        h = h_ref[...]
        for hd in range(tn // RET_DK):
            r = rotated_head(h, hd, RET_DK ** -0.5)
            for ch in range(tm // CHUNK):
                kt_ref[ch, hd * RET_DK:(hd + 1) * RET_DK, :] = r[ch * CHUNK:(ch + 1) * CHUNK, :].T.astype(BF16)

    @pl.when(j >= n_q + n_k)
    def _():
        h = h_ref[...]
        for hd in range(tn // RET_DK):
            cols = slice(hd * RET_DK, (hd + 1) * RET_DK)
            o_ref[:, cols] = _dot(h, w_ref[:, cols]).astype(BF16)


def _inproj_ret(cfg, t, mods, g6, w_all, inst, rope, tm=1024, tn=1024):
    n_tok, d = t.shape
    n_proj = w_all.shape[2]
    tm = _tile(cfg, tm)
    row = _mod_row(cfg, tm)
    lat_tiles = cfg.n_lat // tm
    per_seq = cfg.seq // tm
    trow = lambda i: jnp.where(i < lat_tiles, i % per_seq, per_seq + i - lat_tiles)
    n_q = RET_QK // tn
    n_k = RET_QK // tn
    pcol = lambda j: jnp.where(j < n_q, j, jnp.maximum(j - n_k, n_q - 1))
    kcol = lambda j: jnp.clip(j - n_q, 0, n_k - 1)
    return pl.pallas_call(
        functools.partial(_inproj_ret_body, n_q=n_q, n_k=n_k),
        grid=(n_tok // tm, n_proj // tn),
        in_specs=[
            pl.BlockSpec((tm, d), lambda i, j: (i, 0)),
            pl.BlockSpec((None, 9, d), lambda i, j: (row(i), 0, 0)),
            pl.BlockSpec((6, d), lambda i, j: (0, 0)),
            pl.BlockSpec((tm, RET_DK), lambda i, j: (trow(i), 0)),
            pl.BlockSpec((tm, RET_DK), lambda i, j: (trow(i), 0)),
            pl.BlockSpec((None, d, tn), lambda i, j: (inst, 0, j)),
        ],
        out_specs=[pl.BlockSpec((tm, tn), lambda i, j: (i, pcol(j))),
                   pl.BlockSpec((tm // CHUNK, tn, CHUNK), lambda i, j: (i, kcol(j), 0))],
        out_shape=[jax.ShapeDtypeStruct((n_tok, n_proj - RET_QK), BF16),
                   jax.ShapeDtypeStruct((n_tok // CHUNK, RET_QK, CHUNK), BF16)],
        scratch_shapes=[pltpu.VMEM((tm, d), BF16)],
        compiler_params=_params("arbitrary", "arbitrary"),
        name="inproj_ret",
    )(t, mods, g6, rope[0], rope[1], w_all)


INPROJ_PIECE = 256


def _after_prologue(x_ref, mod_ref, g_ref, h_ref, project):
    j = pl.program_id(1)

    @pl.when(j == 0)
    def _():
        _modulate_rows(x_ref, mod_ref, g_ref, h_ref, 1)
        project()

    @pl.when(j > 0)
    def _():
        project()


def _inproj_gelu_body(x_ref, mod_ref, g_ref, w_ref, o_ref, h_ref):
    def project():
        h = h_ref[...]
        for s in range(o_ref.shape[1] // INPROJ_PIECE):
            cols = slice(s * INPROJ_PIECE, (s + 1) * INPROJ_PIECE)
            o_ref[:, cols] = _gelu_tanh(_wdot(h, w_ref, cols)).astype(BF16)

    _after_prologue(x_ref, mod_ref, g_ref, h_ref, project)


def _inproj_glu_body(x_ref, mod_ref, g_ref, wa_ref, wb_ref, o_ref, h_ref):
    def project():
        h = h_ref[...]
        for s in range(o_ref.shape[1] // INPROJ_PIECE):
            cols = slice(s * INPROJ_PIECE, (s + 1) * INPROJ_PIECE)
            o_ref[:, cols] = (_dot(h, wa_ref[:, cols]) * jax.nn.sigmoid(_dot(h, wb_ref[:, cols]))).astype(BF16)

    _after_prologue(x_ref, mod_ref, g_ref, h_ref, project)


def _inproj(cfg, kind, t, mods, g6, w, tm=1024, tn=1024):
    n_tok, d = t.shape
    tm = _tile(cfg, tm)
    row = _mod_row(cfg, tm)
    common = [
        pl.BlockSpec((tm, d), lambda i, j: (i, 0)),
        pl.BlockSpec((None, 9, d), lambda i, j: (row(i), 0, 0)),
        pl.BlockSpec((6, d), lambda i, j: (0, 0)),
    ]
    if kind == "gelu":
        n_out = w.shape[1]
        body = _inproj_gelu_body
        specs = common + [pl.BlockSpec((d, tn), lambda i, j: (0, j))]
        args = (t, mods, g6, w)
    else:
        n_out = w.shape[1] // 2
        nb = n_out // tn
        body = _inproj_glu_body
        specs = common + [pl.BlockSpec((d, tn), lambda i, j: (0, j)),
                          pl.BlockSpec((d, tn), lambda i, j: (0, nb + j))]
        args = (t, mods, g6, w, w)
    return pl.pallas_call(
        body,
        grid=(n_tok // tm, n_out // tn),
        in_specs=specs,
        out_specs=pl.BlockSpec((tm, tn), lambda i, j: (i, j)),
        out_shape=jax.ShapeDtypeStruct((n_tok, n_out), BF16),
        scratch_shapes=[pltpu.VMEM((tm, d), BF16)],
        compiler_params=_params("arbitrary", "arbitrary"),
        name="inproj_" + kind,
    )(*args)


SCAN_CHUNK = 2 * CHUNK


def _ret_body(dec_ref, gn_ref, ql_ref, ktl_ref, vl_ref, gl_ref, qc_ref, ktc_ref, vc_ref, gc_ref, ol_ref, oc_ref,
              sf_ref, sb_ref, af_ref, ab_ref, *, n_lat_chunks, n_ctx_chunks):
    c = SCAN_CHUNK
    sub = c // CHUNK
    lgf = dec_ref[0:1, 0:1]
    lgb = dec_ref[1:2, 0:1]
    diff = (lax.broadcasted_iota(jnp.int32, (c, c), 0) - lax.broadcasted_iota(jnp.int32, (c, c), 1)).astype(F32)
    dmat = jnp.where(diff >= 0.0, jnp.exp(jnp.maximum(diff, 0.0) * lgf), jnp.exp(jnp.maximum(-diff, 0.0) * lgb))
    pos = lax.broadcasted_iota(jnp.int32, (c, 1), 0).astype(F32)
    lane = lax.broadcasted_iota(jnp.int32, (1, c), 1).astype(F32)
    qdec_f = jnp.exp((pos + 1.0) * lgf)
    kdec_f = jnp.exp((c - 1.0 - lane) * lgf)
    cdec_f = jnp.exp(c * lgf)
    qdec_b = jnp.exp((c - pos) * lgb)
    kdec_b = jnp.exp(lane * lgb)
    cdec_b = jnp.exp(c * lgb)

    def kt_chunk(kt_ref, i):
        return jnp.concatenate([kt_ref[i * sub + s] for s in range(sub)], axis=1)

    def pair(q_ref, kt_ref, v_ref, cf, rf, cb, rb):
        q = q_ref[rf, :]
        kt = kt_chunk(kt_ref, cf)
        v = v_ref[rf, :]
        a = _dot(q, kt) * dmat
        af_ref[rf, :] = _dot(a.astype(BF16), v) + _dot(q, sf_ref[...].astype(BF16)) * qdec_f
        sf_ref[...] = sf_ref[...] * cdec_f + _dot((kt.astype(F32) * kdec_f).astype(BF16), v)
        q = q_ref[rb, :]
        kt = kt_chunk(kt_ref, cb)
        v = v_ref[rb, :]
        ab_ref[rb, :] = _dot(q, sb_ref[...].astype(BF16)) * qdec_b
        sb_ref[...] = sb_ref[...] * cdec_b + _dot((kt.astype(F32) * kdec_b).astype(BF16), v)

    def finish(g_ref, o_ref, rows):
        o = af_ref[rows, :] + ab_ref[rows, :]
        o_ref[rows, :] = (_silu(g_ref[rows, :].astype(F32)) * (_standardize(o) * gn_ref[...])).astype(BF16)

    def chunk_rows(i):
        return pl.ds(pl.multiple_of(i * c, c), c)

    sf_ref[...] = jnp.zeros_like(sf_ref)
    sb_ref[...] = jnp.zeros_like(sb_ref)
    for i in range(n_ctx_chunks):
        ib = n_ctx_chunks - 1 - i
        pair(qc_ref, ktc_ref, vc_ref, i, slice(i * c, (i + 1) * c), ib, slice(ib * c, (ib + 1) * c))
    for i in range(n_ctx_chunks):
        finish(gc_ref, oc_ref, slice(i * c, (i + 1) * c))

    def scan_step(i, carry):
        ib = n_lat_chunks - 1 - i
        pair(ql_ref, ktl_ref, vl_ref, i, chunk_rows(i), ib, chunk_rows(ib))
        return carry

    lax.fori_loop(0, n_lat_chunks, scan_step, 0)

    def finish_step(i, carry):
        finish(gl_ref, ol_ref, chunk_rows(i))
        return carry

    lax.fori_loop(0, n_lat_chunks, finish_step, 0)


def _ret_scan(cfg, p, kt, dec, gn_g):
    assert cfg.seq % SCAN_CHUNK == 0 and cfg.ctx % SCAN_CHUNK == 0
    v_off = RET_QK // RET_DV
    g_off = v_off + RET_V // RET_DV
    ctx0 = cfg.n_lat // cfg.ctx
    return pl.pallas_call(
        functools.partial(_ret_body, n_lat_chunks=cfg.seq // SCAN_CHUNK, n_ctx_chunks=cfg.ctx // SCAN_CHUNK),
        grid=(cfg.batch, RET_HEADS),
        in_specs=[
            pl.BlockSpec((None, 8, LANES), lambda b, h: (h, 0, 0)),
            pl.BlockSpec((1, RET_DV), lambda b, h: (0, h)),
            pl.BlockSpec((cfg.seq, RET_DK), lambda b, h: (b, h)),
            pl.BlockSpec((cfg.seq // CHUNK, RET_DK, CHUNK), lambda b, h: (b, h, 0)),
            pl.BlockSpec((cfg.seq, RET_DV), lambda b, h: (b, v_off + h)),
            pl.BlockSpec((cfg.seq, RET_DV), lambda b, h: (b, g_off + h)),
            pl.BlockSpec((cfg.ctx, RET_DK), lambda b, h: (ctx0 + b, h)),
            pl.BlockSpec((cfg.ctx // CHUNK, RET_DK, CHUNK), lambda b, h: (ctx0 + b, h, 0)),
            pl.BlockSpec((cfg.ctx, RET_DV), lambda b, h: (ctx0 + b, v_off + h)),
            pl.BlockSpec((cfg.ctx, RET_DV), lambda b, h: (ctx0 + b, g_off + h)),
        ],
        out_specs=[pl.BlockSpec((cfg.seq, RET_DV), lambda b, h: (b, h)),
                   pl.BlockSpec((cfg.ctx, RET_DV), lambda b, h: (b, h))],
        out_shape=[jax.ShapeDtypeStruct((cfg.n_lat, RET_V), BF16),
                   jax.ShapeDtypeStruct((cfg.n_ctx, RET_V), BF16)],
        scratch_shapes=[pltpu.VMEM((RET_DK, RET_DV), F32), pltpu.VMEM((RET_DK, RET_DV), F32),
                        pltpu.VMEM((cfg.seq, RET_DV), F32), pltpu.VMEM((cfg.seq, RET_DV), F32)],
        compiler_params=_params("arbitrary", "arbitrary"),
        name="ret_scan",
    )(dec, gn_g, p, kt, p, p, p, kt, p, p)


def _outproj_body(al_ref, ac_ref, w_ref, x_ref, mod_ref, g_ref, o_ref, rstd_ref, *, n_k, lat_tiles):
    i = pl.program_id(0)
    kk = pl.program_id(1)

    @pl.when(kk == 0)
    def _():
        o_ref[...] = jnp.zeros_like(o_ref)

    @pl.when(i < lat_tiles)
    def _():
        o_ref[...] += _dot(al_ref[...], w_ref[...])

    @pl.when(i >= lat_tiles)
    def _():
        o_ref[...] += _dot(ac_ref[...], w_ref[...])

    @pl.when(kk == n_k - 1)
    def _():
        _gated_residual_rows(x_ref, o_ref, mod_ref, g_ref, rstd_ref, 1, 1.0)


def _outproj(cfg, a_lat, a_ctx, w_all, inst, t, mods, g6, n_rows, tm=1024, kc=1024):
    d = t.shape[1]
    tm = _tile(cfg, tm)
    n_k = a_lat.shape[1] // kc
    row = _mod_row(cfg, tm)
    lat_tiles = cfg.n_lat // tm
    lat_idx = lambda i, k: (jnp.minimum(i, lat_tiles - 1), jnp.where(i < lat_tiles, k, n_k - 1))
    ctx_idx = lambda i, k: (jnp.maximum(i - lat_tiles, 0), jnp.where(i < lat_tiles, 0, k))
    return pl.pallas_call(
        functools.partial(_outproj_body, n_k=n_k, lat_tiles=lat_tiles),
        grid=(n_rows // tm, n_k),
        in_specs=[
            pl.BlockSpec((tm, kc), lat_idx),
            pl.BlockSpec((tm, kc), ctx_idx),
            pl.BlockSpec((None, kc, d), lambda i, k: (inst, k, 0)),
            pl.BlockSpec((tm, d), lambda i, k: (i, 0)),
            pl.BlockSpec((None, 9, d), lambda i, k: (row(i), 0, 0)),
            pl.BlockSpec((6, d), lambda i, k: (0, 0)),
        ],
        out_specs=pl.BlockSpec((tm, d), lambda i, k: (i, 0)),
        out_shape=jax.ShapeDtypeStruct((n_rows, d), F32),
        scratch_shapes=[pltpu.VMEM((tm, 1), F32)],
        compiler_params=_params("arbitrary", "arbitrary"),
        name="outproj",
    )(a_lat, a_ctx, w_all, t, mods, g6)


def _gmlp_body(vfull_ref, v_ref, u_ref, lng_ref, lnb_ref, ws_ref, bs_ref, w_ref, x_ref, mod_ref, g_ref,
               o_ref, mean_ref, rstd_ref, gated_ref, *, n_groups):
    grp = pl.program_id(1)
    tm = v_ref.shape[0]

    @pl.when(grp == 0)
    def _():
        v = vfull_ref[...].astype(F32)
        mu = jnp.mean(v, axis=-1, keepdims=True)
        vc = v - mu
        mean_ref[...] = mu
        rstd_ref[...] = lax.rsqrt(jnp.mean(vc * vc, axis=-1, keepdims=True) + EPS)
        o_ref[...] = jnp.zeros_like(o_ref)

    vn = ((v_ref[...].astype(F32) - mean_ref[...]) * rstd_ref[...] * lng_ref[...] + lnb_ref[...]).astype(BF16)
    ws = ws_ref[...]
    bias = bs_ref[:, 0:1]
    for i in range(tm // CHUNK):
        sl = slice(i * CHUNK, (i + 1) * CHUNK)
        mixed = _dot(ws, vn[sl, :]) + bias
        gated_ref[sl, :] = (u_ref[sl, :].astype(F32) * mixed).astype(BF16)
    o_ref[...] += _dot(gated_ref[...], w_ref[...])

    @pl.when(grp == n_groups - 1)
    def _():
        _gated_residual_rows(x_ref, o_ref, mod_ref, g_ref, rstd_ref, 1, 1.0)


def _gmlp_mix(cfg, z, ln_g, ln_b, w_s, b_s, w_out, t, mods, g6, tm=512):
    n_tok, d = t.shape
    tm = _tile(cfg, tm)
    ge = GMLP_GE
    row = _mod_row(cfg, tm)
    return pl.pallas_call(
        functools.partial(_gmlp_body, n_groups=GMLP_GROUPS),
        grid=(n_tok // tm, GMLP_GROUPS),
        in_specs=[
            pl.BlockSpec((tm, GMLP_E), lambda i, g: (i, 1)),
            pl.BlockSpec((tm, ge), lambda i, g: (i, GMLP_GROUPS + g)),
            pl.BlockSpec((tm, ge), lambda i, g: (i, g)),
            pl.BlockSpec((1, ge), lambda i, g: (0, g)),
            pl.BlockSpec((1, ge), lambda i, g: (0, g)),
            pl.BlockSpec((None, CHUNK, CHUNK), lambda i, g: (g, 0, 0)),
            pl.BlockSpec((None, CHUNK, LANES), lambda i, g: (g, 0, 0)),
            pl.BlockSpec((ge, d), lambda i, g: (g, 0)),
            pl.BlockSpec((tm, d), lambda i, g: (i, 0)),
            pl.BlockSpec((None, 9, d), lambda i, g: (row(i), 0, 0)),
            pl.BlockSpec((6, d), lambda i, g: (0, 0)),
        ],
        out_specs=pl.BlockSpec((tm, d), lambda i, g: (i, 0)),
        out_shape=jax.ShapeDtypeStruct((n_tok, d), F32),
        scratch_shapes=[pltpu.VMEM((tm, 1), F32), pltpu.VMEM((tm, 1), F32), pltpu.VMEM((tm, ge), BF16)],
        compiler_params=_params("arbitrary", "arbitrary"),
        name="gmlp_mix",
    )(z, z, z, ln_g, ln_b, w_s, b_s, w_out, t, mods, g6)


CONV_TM = 256


def _conv_body(prev_ref, cur_ref, next_ref, wdw_ref, bdw_ref, lng_ref, lnb_ref, w_ref, x_ref, mod_ref, g_ref,
               o_ref, zs_ref, cz_ref, h_ref, rstd_ref, *, lat_tiles, per_seq):
    i = pl.program_id(0)
    tm = cur_ref.shape[0]
    n_strips = cur_ref.shape[1] // LANES
    hal = CONV_HALO
    pad = CONV_K // 2
    is_ctx = i >= lat_tiles
    first = jnp.logical_or(is_ctx, i % per_seq == 0)
    last = jnp.logical_or(is_ctx, i % per_seq == per_seq - 1)
    pscale = jnp.where(first, 0.0, 1.0)
    nscale = jnp.where(last, 0.0, 1.0)
    for s in range(n_strips):
        ls = slice(s * LANES, (s + 1) * LANES)
        zs_ref[s, 0:hal, :] = prev_ref[:, ls].astype(F32) * pscale
        zs_ref[s, hal:hal + tm, :] = cur_ref[:, ls].astype(F32)
        zs_ref[s, hal + tm:hal + tm + hal, :] = next_ref[:, ls].astype(F32) * nscale

    def strip(s, carry):
        acc = jnp.zeros((tm, LANES), F32) + bdw_ref[s]
        for k in range(CONV_K):
            acc = acc + wdw_ref[s, k:k + 1, :] * zs_ref[s, hal - pad + k:hal - pad + k + tm, :]
        cz_ref[s] = acc
        return carry

    lax.fori_loop(0, n_strips, strip, 0)

    tot = jnp.zeros((tm, 1), F32)
    for s in range(n_strips):
        tot = tot + jnp.sum(cz_ref[s], axis=-1, keepdims=True)
    mu = tot / (n_strips * LANES)
    sq = jnp.zeros((tm, 1), F32)
    for s in range(n_strips):
        dlt = cz_ref[s] - mu
        sq = sq + jnp.sum(dlt * dlt, axis=-1, keepdims=True)
    rstd = lax.rsqrt(sq / (n_strips * LANES) + EPS)
    for s in range(n_strips):
        ls = slice(s * LANES, (s + 1) * LANES)
        y = (cz_ref[s] - mu) * rstd * lng_ref[:, ls] + lnb_ref[:, ls]
        h_ref[:, ls] = _silu(y).astype(BF16)
    o_ref[...] = _dot(h_ref[...], w_ref[...])
    _gated_residual_rows(x_ref, o_ref, mod_ref, g_ref, rstd_ref, 1, 1.0)


def _conv_mix(cfg, z, w_dw, b_dw, ln_g, ln_b, w_pw2, t, mods, g6):
    n_tok, d = t.shape
    tm = CONV_TM
    assert cfg.ctx == tm
    hal = CONV_HALO
    n_strips = d // LANES
    hb = tm // hal
    n_hal = n_tok // hal
    row = _mod_row(cfg, tm)
    return pl.pallas_call(
        functools.partial(_conv_body, lat_tiles=cfg.n_lat // tm, per_seq=cfg.seq // tm),
        grid=(n_tok // tm,),
        in_specs=[
            pl.BlockSpec((hal, d), lambda i: (jnp.maximum(i * hb - 1, 0), 0)),
            pl.BlockSpec((tm, d), lambda i: (i, 0)),
            pl.BlockSpec((hal, d), lambda i: (jnp.minimum((i + 1) * hb, n_hal - 1), 0)),
            pl.BlockSpec((n_strips, 32, LANES), lambda i: (0, 0, 0)),
            pl.BlockSpec((n_strips, 1, LANES), lambda i: (0, 0, 0)),
            pl.BlockSpec((1, d), lambda i: (0, 0)),
            pl.BlockSpec((1, d), lambda i: (0, 0)),
            pl.BlockSpec((d, d), lambda i: (0, 0)),
            pl.BlockSpec((tm, d), lambda i: (i, 0)),
            pl.BlockSpec((None, 9, d), lambda i: (row(i), 0, 0)),
            pl.BlockSpec((6, d), lambda i: (0, 0)),
        ],
        out_specs=pl.BlockSpec((tm, d), lambda i: (i, 0)),
        out_shape=jax.ShapeDtypeStruct((n_tok, d), F32),
        scratch_shapes=[pltpu.VMEM((n_strips, tm + 2 * hal, LANES), F32),
                        pltpu.VMEM((n_strips, tm, LANES), F32),
                        pltpu.VMEM((tm, d), BF16), pltpu.VMEM((tm, 1), F32)],
        compiler_params=_params("arbitrary"),
        name="conv_mix",
    )(z, z, z, w_dw, b_dw, ln_g, ln_b, w_pw2, t, mods, g6)


def _rope_tables(cfg, n_ident):
    quarter = RET_DK // 4
    pos = jnp.arange(cfg.seq, dtype=jnp.int32)
    rows = (pos // GRID_W).astype(F32)
    cols = (pos % GRID_W).astype(F32)
    inv = ROPE_BASE ** (-jnp.arange(quarter, dtype=F32) / quarter)
    ar = rows[:, None] * inv[None, :]
    ac = cols[:, None] * inv[None, :]
    cos = jnp.concatenate([jnp.cos(ar), jnp.cos(ar), jnp.cos(ac), jnp.cos(ac)], axis=1)
    sin = jnp.concatenate([-jnp.sin(ar), jnp.sin(ar), -jnp.sin(ac), jnp.sin(ac)], axis=1)
    cos = jnp.concatenate([cos, jnp.ones((n_ident, RET_DK), F32)], axis=0)
    sin = jnp.concatenate([sin, jnp.zeros((n_ident, RET_DK), F32)], axis=0)
    return cos, sin


def _forward(cfg, x, c, ctx, c_ctx, ada_w, ada_b, norm_g, ffn_w_in, ffn_w_out, ret_w_in, ret_w_out,
             ret_decay_logit, ret_gn_g, gmlp_w_in, gmlp_ln_g, gmlp_ln_b, gmlp_w_s, gmlp_b_s, gmlp_w_out,
             conv_w_pw1, conv_w_dw, conv_b_dw, conv_ln_g, conv_ln_b, conv_w_pw2):
    d = x.shape[-1]
    depth = ada_w.shape[0]
    t = x.reshape(cfg.n_lat, d)
    t_ctx = ctx.reshape(cfg.n_ctx, d)
    cc = jnp.zeros((ADA_ROWS, d), F32).at[:cfg.batch].set(c).at[cfg.batch].set(c_ctx)
    mods_all = _ada_table(cc, ada_w, ada_b)
    ret_w_in_bf = ret_w_in.astype(BF16)
    ret_w_out_bf = ret_w_out.astype(BF16)
    rope = _rope_tables(cfg, cfg.n_ctx)

    for i in range(depth):
        kind = i % N_MIXERS
        inst = i // N_MIXERS
        last = i == depth - 1
        mods = mods_all[i]
        g6 = norm_g[i].reshape(6, d)
        n_after = cfg.n_lat if last else cfg.n_tok
        t = _ffn_sublayer(cfg, t, mods, g6, ffn_w_in, ffn_w_out, i, 0, 0, cfg.n_tok,
                          t_ctx=t_ctx if i == 0 else None)
        if kind == 0:
            p, kt = _inproj_ret(cfg, t, mods, g6, ret_w_in_bf, inst, rope)
            log_g = jax.nn.log_sigmoid(ret_decay_logit[inst].astype(F32))
            dec = jnp.zeros((RET_HEADS, 8, LANES), F32).at[:, 0:2, :].set(
                jnp.broadcast_to(log_g.T[:, :, None], (RET_HEADS, 2, LANES)))
            a_lat, a_ctx = _ret_scan(cfg, p, kt, dec, ret_gn_g[inst].reshape(1, RET_V))
            t = _outproj(cfg, a_lat, a_ctx, ret_w_out_bf, inst, t, mods, g6, n_after)
        elif kind == 1:
            z = _inproj(cfg, "gelu", t, mods, g6, gmlp_w_in[inst])
            b_s = jnp.broadcast_to(gmlp_b_s[inst][:, :, None], (GMLP_GROUPS, CHUNK, LANES))
            t = _gmlp_mix(cfg, z, gmlp_ln_g[inst].reshape(1, GMLP_E), gmlp_ln_b[inst].reshape(1, GMLP_E),
                          gmlp_w_s[inst].astype(BF16), b_s, gmlp_w_out[inst].astype(BF16), t, mods, g6)
        else:
            z = _inproj(cfg, "glu", t, mods, g6, conv_w_pw1[inst].astype(BF16))
            n_strips = d // LANES
            w_dw = jnp.zeros((32, d), F32).at[:CONV_K].set(conv_w_dw[inst])
            w_dw = w_dw.reshape(32, n_strips, LANES).transpose(1, 0, 2)
            t = _conv_mix(cfg, z, w_dw, conv_b_dw[inst].reshape(n_strips, 1, LANES),
                          conv_ln_g[inst].reshape(1, d), conv_ln_b[inst].reshape(1, d),
                          conv_w_pw2[inst].astype(BF16), t, mods, g6)
        t = _ffn_sublayer(cfg, t, mods, g6, ffn_w_in, ffn_w_out, i, 1, 2, n_after)
    return t[:cfg.n_lat].reshape(x.shape)


def kernel(x, c, ctx, c_ctx, ada_w, ada_b, norm_g, ffn_w_in, ffn_w_out, ret_w_in, ret_w_out, ret_decay_logit,
           ret_gn_g, gmlp_w_in, gmlp_ln_g, gmlp_ln_b, gmlp_w_s, gmlp_b_s, gmlp_w_out, conv_w_pw1, conv_w_dw,
           conv_b_dw, conv_ln_g, conv_ln_b, conv_w_pw2):
    cfg = _Cfg(batch=x.shape[0], seq=x.shape[1], ctx=ctx.shape[1])
    return _forward(cfg, x, c, ctx, c_ctx, ada_w, ada_b, norm_g, ffn_w_in, ffn_w_out, ret_w_in, ret_w_out,
                    ret_decay_logit, ret_gn_g, gmlp_w_in, gmlp_ln_g, gmlp_ln_b, gmlp_w_s, gmlp_b_s, gmlp_w_out,
                    conv_w_pw1, conv_w_dw, conv_b_dw, conv_ln_g, conv_ln_b, conv_w_pw2)
```

```python
import functools
from typing import NamedTuple

import jax
import jax.numpy as jnp
from jax import lax
from jax.experimental import pallas as pl
from jax.experimental.pallas import tpu as pltpu

F32 = jnp.float32
BF16 = jnp.bfloat16

D_MODEL = 2048
DEPTH = 4
GRID_W = 64
N_MIXERS = 3
D_FF = 5632
RET_HEADS = 8
RET_DK = D_MODEL // RET_HEADS
RET_DV = 2 * RET_DK
RET_QK = RET_HEADS * RET_DK
RET_V = RET_HEADS * RET_DV
RET_IN = 2 * RET_QK + 2 * RET_V
CHUNK = 128
ROPE_BASE = 10000.0
GMLP_GROUPS = 8
GMLP_E = 3 * D_MODEL
GMLP_GE = GMLP_E // GMLP_GROUPS
CONV_K = 31
CONV_HALO = 16
EPS = 1e-6
ADA_ROWS = 8
LANES = 128

V7X_VMEM_BYTES = 64 * 1024 * 1024
VMEM_LIMIT = V7X_VMEM_BYTES - 4 * 1024 * 1024


class _Cfg(NamedTuple):
    batch: int
    seq: int
    ctx: int

    @property
    def n_lat(self):
        return self.batch * self.seq

    @property
    def n_ctx(self):
        return self.batch * self.ctx

    @property
    def n_tok(self):
        return self.n_lat + self.n_ctx


def _params(*sem):
    return pltpu.CompilerParams(dimension_semantics=sem, vmem_limit_bytes=VMEM_LIMIT)


def _tile(cfg, want):
    tm = want
    while cfg.seq % tm or cfg.n_ctx % tm:
        tm //= 2
    return tm


def _mod_row(cfg, tm):
    lat_tiles = cfg.n_lat // tm
    per_seq = cfg.seq // tm
    return lambda i: jnp.where(i < lat_tiles, i // per_seq, cfg.batch)


NORM_ROWS = 16


def _modulate_rows(x_ref, mod_ref, g_ref, h_ref, sub):
    x = x_ref[...]
    xn = x * lax.rsqrt(jnp.mean(x * x, axis=-1, keepdims=True) + EPS)
    h = (xn * g_ref[2 * sub:2 * sub + 1, :]) * (1.0 + mod_ref[3 * sub + 1:3 * sub + 2, :])
    h_ref[...] = (h + mod_ref[3 * sub:3 * sub + 1, :]).astype(BF16)


def _gated_residual_rows(x_ref, o_ref, mod_ref, g_ref, rstd_ref, sub, weight):
    blocks = [slice(i, i + NORM_ROWS) for i in range(0, x_ref.shape[0], NORM_ROWS)]
    for rows in blocks:
        v = o_ref[rows, :]
        rstd_ref[rows, :] = lax.rsqrt(jnp.mean(v * v, axis=-1, keepdims=True) + EPS)
    for rows in blocks:
        yn = o_ref[rows, :] * rstd_ref[rows, :]
        gate = weight * mod_ref[3 * sub + 2:3 * sub + 3, :]
        o_ref[rows, :] = x_ref[rows, :] + gate * (yn * g_ref[2 * sub + 1:2 * sub + 2, :])


def _standardize(x):
    xc = x - jnp.mean(x, axis=-1, keepdims=True)
    return xc * lax.rsqrt(jnp.mean(xc * xc, axis=-1, keepdims=True) + EPS)


def _silu(x):
    return x * jax.nn.sigmoid(x)


def _gelu_tanh(x):
    return 0.5 * x * (1.0 + jnp.tanh(0.7978845608028654 * (x + 0.044715 * (x * x * x))))


def _dot(a, b):
    return jnp.dot(a, b, preferred_element_type=F32)


def _wdot(a, w_ref, cols=slice(None)):
    return _dot(a, w_ref[:, cols].astype(BF16))


def _ada_body(c_ref, w_ref, b_ref, o_ref):
    s = _silu(c_ref[...]).astype(BF16)
    o_ref[...] = _dot(s, w_ref[...].astype(BF16)) + b_ref[...]


def _ada_table(cc, ada_w, ada_b):
    depth, d, n = ada_w.shape
    tn = 1024
    out = pl.pallas_call(
        _ada_body,
        grid=(depth, n // tn),
        in_specs=[
            pl.BlockSpec((ADA_ROWS, d), lambda i, j: (0, 0)),
            pl.BlockSpec((None, d, tn), lambda i, j: (i, 0, j)),
            pl.BlockSpec((None, 1, tn), lambda i, j: (i, 0, j)),
        ],
        out_specs=pl.BlockSpec((None, ADA_ROWS, tn), lambda i, j: (i, 0, j)),
        out_shape=jax.ShapeDtypeStruct((depth, ADA_ROWS, n), F32),
        compiler_params=_params("arbitrary", "arbitrary"),
        name="ada_table",
    )(cc, ada_w, ada_b.reshape(depth, 1, n))
    return out.reshape(depth, ADA_ROWS, n // d, d)


FIRST_TILES = 2


def _ffn_chunk(h_ref, wa, wb, wo):
    h = h_ref[...]
    act = (_silu(_dot(h, wa)) * _dot(h, wb)).astype(BF16)
    return _dot(act, wo)


def _ffn_first_body(x_ref, mod_ref, g_ref, wa32_ref, wb32_ref, wo32_ref, o_ref, wa_ref, wb_ref, wo_ref,
                    h_ref, rstd_ref, *, sub, n_ff):
    j = pl.program_id(0)

    @pl.when(j == 0)
    def _():
        _modulate_rows(x_ref, mod_ref, g_ref, h_ref, sub)
        o_ref[...] = jnp.zeros_like(o_ref)

    wa_ref[...] = wa32_ref[...].astype(BF16)
    wb_ref[...] = wb32_ref[...].astype(BF16)
    wo_ref[...] = wo32_ref[...].astype(BF16)
    o_ref[...] += _ffn_chunk(h_ref, wa_ref[...], wb_ref[...], wo_ref[...])

    @pl.when(j == n_ff - 1)
    def _():
        _gated_residual_rows(x_ref, o_ref, mod_ref, g_ref, rstd_ref, sub, 0.5)


def _ffn_body(x_ref, mod_ref, g_ref, wa_ref, wb_ref, wo_ref, first_ref, o_ref, h_ref, rstd_ref, *,
              sub, n_ff, n_first):
    i = pl.program_id(0)
    j = pl.program_id(1)

    @pl.when(jnp.logical_and(i < n_first, j == 0))
    def _():
        o_ref[...] = first_ref[...]

    @pl.when(i >= n_first)
    def _():
        @pl.when(j == 0)
        def _():
            _modulate_rows(x_ref, mod_ref, g_ref, h_ref, sub)
            o_ref[...] = _ffn_chunk(h_ref, wa_ref[...], wb_ref[...], wo_ref[...])

        @pl.when(jnp.logical_and(j > 0, j < n_ff - 1))
        def _():
            o_ref[...] += _ffn_chunk(h_ref, wa_ref[...], wb_ref[...], wo_ref[...])

        @pl.when(j == n_ff - 1)
        def _():
            o_ref[...] += _ffn_chunk(h_ref, wa_ref[...], wb_ref[...], wo_ref[...])
            _gated_residual_rows(x_ref, o_ref, mod_ref, g_ref, rstd_ref, sub, 0.5)


def _ffn_split_body(xl_ref, xc_ref, *refs, lat_tiles, **kw):
    i = pl.program_id(0)

    @pl.when(i < lat_tiles)
    def _():
        _ffn_body(xl_ref, *refs, **kw)

    @pl.when(i >= lat_tiles)
    def _():
        _ffn_body(xc_ref, *refs, **kw)


def _ffn_sublayer(cfg, t, mods, g6, w_in, w_out, layer, which, sub, n_rows, t_ctx=None,
                  tm=512, fc=512, fc_first=256):
    d = t.shape[1]
    tm = _tile(cfg, tm)
    f = w_out.shape[2]
    row = _mod_row(cfg, tm)

    tm_first = FIRST_TILES * tm
    assert tm_first <= cfg.seq
    n1 = f // fc_first
    resident = pl.BlockSpec(memory_space=pltpu.VMEM)
    first, wa, wb, wo = pl.pallas_call(
        functools.partial(_ffn_first_body, sub=sub, n_ff=n1),
        grid=(n1,),
        in_specs=[
            resident,
            pl.BlockSpec((None, 9, d), lambda j: (0, 0, 0)),
            pl.BlockSpec((6, d), lambda j: (0, 0)),
            pl.BlockSpec((None, None, d, fc_first), lambda j: (layer, which, 0, j)),
            pl.BlockSpec((None, None, d, fc_first), lambda j: (layer, which, 0, n1 + j)),
            pl.BlockSpec((None, None, fc_first, d), lambda j: (layer, which, j, 0)),
        ],
        out_specs=[resident,
                   pl.BlockSpec((d, fc_first), lambda j: (0, j)),
                   pl.BlockSpec((d, fc_first), lambda j: (0, j)),
                   pl.BlockSpec((fc_first, d), lambda j: (j, 0))],
        out_shape=[jax.ShapeDtypeStruct((tm_first, d), F32), jax.ShapeDtypeStruct((d, f), BF16),
                   jax.ShapeDtypeStruct((d, f), BF16), jax.ShapeDtypeStruct((f, d), BF16)],
        scratch_shapes=[pltpu.VMEM((tm_first, d), BF16), pltpu.VMEM((tm_first, 1), F32)],
        compiler_params=_params("arbitrary"),
        name="ffn_first_tile",
    )(t[:tm_first], mods, g6, w_in, w_in, w_out)

    n_ff = f // fc
    chunk = lambda i, j: jnp.where(i < FIRST_TILES, 0, j)
    kw = dict(sub=sub, n_ff=n_ff, n_first=FIRST_TILES)
    if t_ctx is None:
        body = functools.partial(_ffn_body, **kw)
        streams = (t,)
        x_specs = [pl.BlockSpec((tm, d), lambda i, j: (i, 0))]
    else:
        lat_tiles = cfg.n_lat // tm
        body = functools.partial(_ffn_split_body, lat_tiles=lat_tiles, **kw)
        streams = (t, t_ctx)
        x_specs = [pl.BlockSpec((tm, d), lambda i, j: (jnp.minimum(i, lat_tiles - 1), 0)),
                   pl.BlockSpec((tm, d), lambda i, j: (jnp.maximum(i - lat_tiles, 0), 0))]
    return pl.pallas_call(
        body,
        grid=(n_rows // tm, n_ff),
        in_specs=x_specs + [
            pl.BlockSpec((None, 9, d), lambda i, j: (row(i), 0, 0)),
            pl.BlockSpec((6, d), lambda i, j: (0, 0)),
            pl.BlockSpec((d, fc), lambda i, j: (0, chunk(i, j))),
            pl.BlockSpec((d, fc), lambda i, j: (0, chunk(i, j))),
            pl.BlockSpec((fc, d), lambda i, j: (chunk(i, j), 0)),
            pl.BlockSpec((tm, d), lambda i, j: (jnp.minimum(i, FIRST_TILES - 1), 0)),
        ],
        out_specs=pl.BlockSpec((tm, d), lambda i, j: (i, 0)),
        out_shape=jax.ShapeDtypeStruct((n_rows, d), F32),
        scratch_shapes=[pltpu.VMEM((tm, d), BF16), pltpu.VMEM((tm, 1), F32)],
        compiler_params=_params("arbitrary", "arbitrary"),
        name="ffn_sublayer",
    )(*streams, mods, g6, wa, wb, wo, first)


def _inproj_ret_body(x_ref, mod_ref, g_ref, cos_ref, sin_ref, w_ref, o_ref, kt_ref, h_ref, *, n_q, n_k):
    j = pl.program_id(1)
    tm, tn = o_ref.shape

    def rotated_head(h, hd, scale):
        a = _dot(h, w_ref[:, hd * RET_DK:(hd + 1) * RET_DK])
        swapped = jnp.concatenate(
            [pltpu.roll(a[:, g * LANES:(g + 1) * LANES], LANES // 2, 1) for g in range(RET_DK // LANES)], axis=1)
        return (a * cos_ref[...] + swapped * sin_ref[...]) * scale

    def q_tile():
        h = h_ref[...]
        for hd in range(tn // RET_DK):
            o_ref[:, hd * RET_DK:(hd + 1) * RET_DK] = rotated_head(h, hd, 1.0).astype(BF16)

    @pl.when(j == 0)
    def _():
        _modulate_rows(x_ref, mod_ref, g_ref, h_ref, 1)
        q_tile()

    @pl.when(jnp.logical_and(j > 0, j < n_q))
    def _():
        q_tile()

    @pl.when(jnp.logical_and(j >= n_q, j < n_q + n_k))
    def _():
        h = h_ref[...]
        for hd in range(tn // RET_DK):
            r = rotated_head(h, hd, RET_DK ** -0.5)
            for ch in range(tm // CHUNK):
                kt_ref[ch, hd * RET_DK:(hd + 1) * RET_DK, :] = r[ch * CHUNK:(ch + 1) * CHUNK, :].T.astype(BF16)

    @pl.when(j >= n_q + n_k)
    def _():
        h = h_ref[...]
        for hd in range(tn // RET_DK):
            cols = slice(hd * RET_DK, (hd + 1) * RET_DK)
            o_ref[:, cols] = _dot(h, w_ref[:, cols]).astype(BF16)


def _inproj_ret(cfg, t, mods, g6, w_all, inst, rope, tm=1024, tn=1024):
    n_tok, d = t.shape
    n_proj = w_all.shape[2]
    tm = _tile(cfg, tm)
    row = _mod_row(cfg, tm)
    lat_tiles = cfg.n_lat // tm
    per_seq = cfg.seq // tm
    trow = lambda i: jnp.where(i < lat_tiles, i % per_seq, per_seq + i - lat_tiles)
    n_q = RET_QK // tn
    n_k = RET_QK // tn
    pcol = lambda j: jnp.where(j < n_q, j, jnp.maximum(j - n_k, n_q - 1))
    kcol = lambda j: jnp.clip(j - n_q, 0, n_k - 1)
    return pl.pallas_call(
        functools.partial(_inproj_ret_body, n_q=n_q, n_k=n_k),
        grid=(n_tok // tm, n_proj // tn),
        in_specs=[
            pl.BlockSpec((tm, d), lambda i, j: (i, 0)),
            pl.BlockSpec((None, 9, d), lambda i, j: (row(i), 0, 0)),
            pl.BlockSpec((6, d), lambda i, j: (0, 0)),
            pl.BlockSpec((tm, RET_DK), lambda i, j: (trow(i), 0)),
            pl.BlockSpec((tm, RET_DK), lambda i, j: (trow(i), 0)),
            pl.BlockSpec((None, d, tn), lambda i, j: (inst, 0, j)),
        ],
        out_specs=[pl.BlockSpec((tm, tn), lambda i, j: (i, pcol(j))),
                   pl.BlockSpec((tm // CHUNK, tn, CHUNK), lambda i, j: (i, kcol(j), 0))],
        out_shape=[jax.ShapeDtypeStruct((n_tok, n_proj - RET_QK), BF16),
                   jax.ShapeDtypeStruct((n_tok // CHUNK, RET_QK, CHUNK), BF16)],
        scratch_shapes=[pltpu.VMEM((tm, d), BF16)],
        compiler_params=_params("arbitrary", "arbitrary"),
        name="inproj_ret",
    )(t, mods, g6, rope[0], rope[1], w_all)


INPROJ_PIECE = 256


def _after_prologue(x_ref, mod_ref, g_ref, h_ref, project):
    j = pl.program_id(1)

    @pl.when(j == 0)
    def _():
        _modulate_rows(x_ref, mod_ref, g_ref, h_ref, 1)
        project()

    @pl.when(j > 0)
    def _():
        project()


def _inproj_gelu_body(x_ref, mod_ref, g_ref, w_ref, o_ref, h_ref):
    def project():
        h = h_ref[...]
        for s in range(o_ref.shape[1] // INPROJ_PIECE):
            cols = slice(s * INPROJ_PIECE, (s + 1) * INPROJ_PIECE)
            o_ref[:, cols] = _gelu_tanh(_wdot(h, w_ref, cols)).astype(BF16)

    _after_prologue(x_ref, mod_ref, g_ref, h_ref, project)


def _inproj_glu_body(x_ref, mod_ref, g_ref, wa_ref, wb_ref, o_ref, h_ref):
    def project():
        h = h_ref[...]
        for s in range(o_ref.shape[1] // INPROJ_PIECE):
            cols = slice(s * INPROJ_PIECE, (s + 1) * INPROJ_PIECE)
            o_ref[:, cols] = (_dot(h, wa_ref[:, cols]) * jax.nn.sigmoid(_dot(h, wb_ref[:, cols]))).astype(BF16)

    _after_prologue(x_ref, mod_ref, g_ref, h_ref, project)


def _inproj(cfg, kind, t, mods, g6, w, tm=1024, tn=1024):
    n_tok, d = t.shape
    tm = _tile(cfg, tm)
    row = _mod_row(cfg, tm)
    common = [
        pl.BlockSpec((tm, d), lambda i, j: (i, 0)),
        pl.BlockSpec((None, 9, d), lambda i, j: (row(i), 0, 0)),
        pl.BlockSpec((6, d), lambda i, j: (0, 0)),
    ]
    if kind == "gelu":
        n_out = w.shape[1]
        body = _inproj_gelu_body
        specs = common + [pl.BlockSpec((d, tn), lambda i, j: (0, j))]
        args = (t, mods, g6, w)
    else:
        n_out = w.shape[1] // 2
        nb = n_out // tn
        body = _inproj_glu_body
        specs = common + [pl.BlockSpec((d, tn), lambda i, j: (0, j)),
                          pl.BlockSpec((d, tn), lambda i, j: (0, nb + j))]
        args = (t, mods, g6, w, w)
    return pl.pallas_call(
        body,
        grid=(n_tok // tm, n_out // tn),
        in_specs=specs,
        out_specs=pl.BlockSpec((tm, tn), lambda i, j: (i, j)),
        out_shape=jax.ShapeDtypeStruct((n_tok, n_out), BF16),
        scratch_shapes=[pltpu.VMEM((tm, d), BF16)],
        compiler_params=_params("arbitrary", "arbitrary"),
        name="inproj_" + kind,
    )(*args)


SCAN_CHUNK = 2 * CHUNK


def _ret_body(dec_ref, gn_ref, ql_ref, ktl_ref, vl_ref, gl_ref, qc_ref, ktc_ref, vc_ref, gc_ref, ol_ref, oc_ref,
              sf_ref, sb_ref, af_ref, ab_ref, *, n_lat_chunks, n_ctx_chunks):
    c = SCAN_CHUNK
    sub = c // CHUNK
    lgf = dec_ref[0:1, 0:1]
    lgb = dec_ref[1:2, 0:1]
    diff = (lax.broadcasted_iota(jnp.int32, (c, c), 0) - lax.broadcasted_iota(jnp.int32, (c, c), 1)).astype(F32)
    dmat = jnp.where(diff >= 0.0, jnp.exp(jnp.maximum(diff, 0.0) * lgf), jnp.exp(jnp.maximum(-diff, 0.0) * lgb))
    pos = lax.broadcasted_iota(jnp.int32, (c, 1), 0).astype(F32)
    lane = lax.broadcasted_iota(jnp.int32, (1, c), 1).astype(F32)
    qdec_f = jnp.exp((pos + 1.0) * lgf)
    kdec_f = jnp.exp((c - 1.0 - lane) * lgf)
    cdec_f = jnp.exp(c * lgf)
    qdec_b = jnp.exp((c - pos) * lgb)
    kdec_b = jnp.exp(lane * lgb)
    cdec_b = jnp.exp(c * lgb)

    def kt_chunk(kt_ref, i):
        return jnp.concatenate([kt_ref[i * sub + s] for s in range(sub)], axis=1)

    def pair(q_ref, kt_ref, v_ref, cf, rf, cb, rb):
        q = q_ref[rf, :]
        kt = kt_chunk(kt_ref, cf)
        v = v_ref[rf, :]
        a = _dot(q, kt) * dmat
        af_ref[rf, :] = _dot(a.astype(BF16), v) + _dot(q, sf_ref[...].astype(BF16)) * qdec_f
        sf_ref[...] = sf_ref[...] * cdec_f + _dot((kt.astype(F32) * kdec_f).astype(BF16), v)
        q = q_ref[rb, :]
        kt = kt_chunk(kt_ref, cb)
        v = v_ref[rb, :]
        ab_ref[rb, :] = _dot(q, sb_ref[...].astype(BF16)) * qdec_b
        sb_ref[...] = sb_ref[...] * cdec_b + _dot((kt.astype(F32) * kdec_b).astype(BF16), v)

    def finish(g_ref, o_ref, rows):
        o = af_ref[rows, :] + ab_ref[rows, :]
        o_ref[rows, :] = (_silu(g_ref[rows, :].astype(F32)) * (_standardize(o) * gn_ref[...])).astype(BF16)

    def chunk_rows(i):
        return pl.ds(pl.multiple_of(i * c, c), c)

    sf_ref[...] = jnp.zeros_like(sf_ref)
    sb_ref[...] = jnp.zeros_like(sb_ref)
    for i in range(n_ctx_chunks):
        ib = n_ctx_chunks - 1 - i
        pair(qc_ref, ktc_ref, vc_ref, i, slice(i * c, (i + 1) * c), ib, slice(ib * c, (ib + 1) * c))
    for i in range(n_ctx_chunks):
        finish(gc_ref, oc_ref, slice(i * c, (i + 1) * c))

    def scan_step(i, carry):
        ib = n_lat_chunks - 1 - i
        pair(ql_ref, ktl_ref, vl_ref, i, chunk_rows(i), ib, chunk_rows(ib))
        return carry

    lax.fori_loop(0, n_lat_chunks, scan_step, 0)

    def finish_step(i, carry):
        finish(gl_ref, ol_ref, chunk_rows(i))
        return carry

    lax.fori_loop(0, n_lat_chunks, finish_step, 0)


def _ret_scan(cfg, p, kt, dec, gn_g):
    assert cfg.seq % SCAN_CHUNK == 0 and cfg.ctx % SCAN_CHUNK == 0
    v_off = RET_QK // RET_DV
    g_off = v_off + RET_V // RET_DV
    ctx0 = cfg.n_lat // cfg.ctx
    return pl.pallas_call(
        functools.partial(_ret_body, n_lat_chunks=cfg.seq // SCAN_CHUNK, n_ctx_chunks=cfg.ctx // SCAN_CHUNK),
        grid=(cfg.batch, RET_HEADS),
        in_specs=[
            pl.BlockSpec((None, 8, LANES), lambda b, h: (h, 0, 0)),
            pl.BlockSpec((1, RET_DV), lambda b, h: (0, h)),
            pl.BlockSpec((cfg.seq, RET_DK), lambda b, h: (b, h)),
            pl.BlockSpec((cfg.seq // CHUNK, RET_DK, CHUNK), lambda b, h: (b, h, 0)),
            pl.BlockSpec((cfg.seq, RET_DV), lambda b, h: (b, v_off + h)),
            pl.BlockSpec((cfg.seq, RET_DV), lambda b, h: (b, g_off + h)),
            pl.BlockSpec((cfg.ctx, RET_DK), lambda b, h: (ctx0 + b, h)),
            pl.BlockSpec((cfg.ctx // CHUNK, RET_DK, CHUNK), lambda b, h: (ctx0 + b, h, 0)),
            pl.BlockSpec((cfg.ctx, RET_DV), lambda b, h: (ctx0 + b, v_off + h)),
            pl.BlockSpec((cfg.ctx, RET_DV), lambda b, h: (ctx0 + b, g_off + h)),
        ],
        out_specs=[pl.BlockSpec((cfg.seq, RET_DV), lambda b, h: (b, h)),
                   pl.BlockSpec((cfg.ctx, RET_DV), lambda b, h: (b, h))],
        out_shape=[jax.ShapeDtypeStruct((cfg.n_lat, RET_V), BF16),
                   jax.ShapeDtypeStruct((cfg.n_ctx, RET_V), BF16)],
        scratch_shapes=[pltpu.VMEM((RET_DK, RET_DV), F32), pltpu.VMEM((RET_DK, RET_DV), F32),
                        pltpu.VMEM((cfg.seq, RET_DV), F32), pltpu.VMEM((cfg.seq, RET_DV), F32)],
        compiler_params=_params("arbitrary", "arbitrary"),
        name="ret_scan",
    )(dec, gn_g, p, kt, p, p, p, kt, p, p)


def _outproj_body(al_ref, ac_ref, w_ref, x_ref, mod_ref, g_ref, o_ref, rstd_ref, *, n_k, lat_tiles):
    i = pl.program_id(0)
    kk = pl.program_id(1)

    def accumulate(a_ref):
        @pl.when(kk == 0)
        def _():
            o_ref[...] = _dot(a_ref[...], w_ref[...])

        @pl.when(jnp.logical_and(kk > 0, kk < n_k - 1))
        def _():
            o_ref[...] += _dot(a_ref[...], w_ref[...])

        @pl.when(kk == n_k - 1)
        def _():
            o_ref[...] += _dot(a_ref[...], w_ref[...])
            _gated_residual_rows(x_ref, o_ref, mod_ref, g_ref, rstd_ref, 1, 1.0)

    @pl.when(i < lat_tiles)
    def _():
        accumulate(al_ref)

    @pl.when(i >= lat_tiles)
    def _():
        accumulate(ac_ref)


def _outproj(cfg, a_lat, a_ctx, w_all, inst, t, mods, g6, n_rows, tm=1024, kc=1024):
    d = t.shape[1]
    tm = _tile(cfg, tm)
    n_k = a_lat.shape[1] // kc
    assert n_k >= 2
    row = _mod_row(cfg, tm)
    lat_tiles = cfg.n_lat // tm
    lat_idx = lambda i, k: (jnp.minimum(i, lat_tiles - 1), jnp.where(i < lat_tiles, k, n_k - 1))
    ctx_idx = lambda i, k: (jnp.maximum(i - lat_tiles, 0), jnp.where(i < lat_tiles, 0, k))
    return pl.pallas_call(
        functools.partial(_outproj_body, n_k=n_k, lat_tiles=lat_tiles),
        grid=(n_rows // tm, n_k),
        in_specs=[
            pl.BlockSpec((tm, kc), lat_idx),
            pl.BlockSpec((tm, kc), ctx_idx),
            pl.BlockSpec((None, kc, d), lambda i, k: (inst, k, 0)),
            pl.BlockSpec((tm, d), lambda i, k: (i, 0)),
            pl.BlockSpec((None, 9, d), lambda i, k: (row(i), 0, 0)),
            pl.BlockSpec((6, d), lambda i, k: (0, 0)),
        ],
        out_specs=pl.BlockSpec((tm, d), lambda i, k: (i, 0)),
        out_shape=jax.ShapeDtypeStruct((n_rows, d), F32),
        scratch_shapes=[pltpu.VMEM((tm, 1), F32)],
        compiler_params=_params("arbitrary", "arbitrary"),
        name="outproj",
    )(a_lat, a_ctx, w_all, t, mods, g6)


def _gmlp_body(vfull_ref, v_ref, u_ref, lng_ref, lnb_ref, ws_ref, bs_ref, w_ref, x_ref, mod_ref, g_ref,
               o_ref, mean_ref, rstd_ref, gated_ref, *, n_groups):
    grp = pl.program_id(1)
    tm = v_ref.shape[0]

    def group_out():
        vn = ((v_ref[...].astype(F32) - mean_ref[...]) * rstd_ref[...] * lng_ref[...] + lnb_ref[...]).astype(BF16)
        ws = ws_ref[...]
        bias = bs_ref[:, 0:1]
        for i in range(tm // CHUNK):
            sl = slice(i * CHUNK, (i + 1) * CHUNK)
            mixed = _dot(ws, vn[sl, :]) + bias
            gated_ref[sl, :] = (u_ref[sl, :].astype(F32) * mixed).astype(BF16)
        return _dot(gated_ref[...], w_ref[...])

    @pl.when(grp == 0)
    def _():
        v = vfull_ref[...].astype(F32)
        mu = jnp.mean(v, axis=-1, keepdims=True)
        vc = v - mu
        mean_ref[...] = mu
        rstd_ref[...] = lax.rsqrt(jnp.mean(vc * vc, axis=-1, keepdims=True) + EPS)
        o_ref[...] = group_out()

    @pl.when(jnp.logical_and(grp > 0, grp < n_groups - 1))
    def _():
        o_ref[...] += group_out()

    @pl.when(grp == n_groups - 1)
    def _():
        o_ref[...] += group_out()
        _gated_residual_rows(x_ref, o_ref, mod_ref, g_ref, rstd_ref, 1, 1.0)


def _gmlp_mix(cfg, z, ln_g, ln_b, w_s, b_s, w_out, t, mods, g6, tm=512):
    n_tok, d = t.shape
    tm = _tile(cfg, tm)
    ge = GMLP_GE
    row = _mod_row(cfg, tm)
    return pl.pallas_call(
        functools.partial(_gmlp_body, n_groups=GMLP_GROUPS),
        grid=(n_tok // tm, GMLP_GROUPS),
        in_specs=[
            pl.BlockSpec((tm, GMLP_E), lambda i, g: (i, 1)),
            pl.BlockSpec((tm, ge), lambda i, g: (i, GMLP_GROUPS + g)),
            pl.BlockSpec((tm, ge), lambda i, g: (i, g)),
            pl.BlockSpec((1, ge), lambda i, g: (0, g)),
            pl.BlockSpec((1, ge), lambda i, g: (0, g)),
            pl.BlockSpec((None, CHUNK, CHUNK), lambda i, g: (g, 0, 0)),
            pl.BlockSpec((None, CHUNK, LANES), lambda i, g: (g, 0, 0)),
            pl.BlockSpec((ge, d), lambda i, g: (g, 0)),
            pl.BlockSpec((tm, d), lambda i, g: (i, 0)),
            pl.BlockSpec((None, 9, d), lambda i, g: (row(i), 0, 0)),
            pl.BlockSpec((6, d), lambda i, g: (0, 0)),
        ],
        out_specs=pl.BlockSpec((tm, d), lambda i, g: (i, 0)),
        out_shape=jax.ShapeDtypeStruct((n_tok, d), F32),
        scratch_shapes=[pltpu.VMEM((tm, 1), F32), pltpu.VMEM((tm, 1), F32), pltpu.VMEM((tm, ge), BF16)],
        compiler_params=_params("arbitrary", "arbitrary"),
        name="gmlp_mix",
    )(z, z, z, ln_g, ln_b, w_s, b_s, w_out, t, mods, g6)


CONV_TM = 256


def _conv_body(prev_ref, cur_ref, next_ref, wdw_ref, bdw_ref, lng_ref, lnb_ref, w_ref, x_ref, mod_ref, g_ref,
               o_ref, zs_ref, cz_ref, h_ref, rstd_ref, *, lat_tiles, per_seq):
    i = pl.program_id(0)
    tm = cur_ref.shape[0]
    n_strips = cur_ref.shape[1] // LANES
    hal = CONV_HALO
    pad = CONV_K // 2
    is_ctx = i >= lat_tiles
    first = jnp.logical_or(is_ctx, i % per_seq == 0)
    last = jnp.logical_or(is_ctx, i % per_seq == per_seq - 1)
    pscale = jnp.where(first, 0.0, 1.0)
    nscale = jnp.where(last, 0.0, 1.0)
    for s in range(n_strips):
        ls = slice(s * LANES, (s + 1) * LANES)
        zs_ref[s, 0:hal, :] = prev_ref[:, ls].astype(F32) * pscale
        zs_ref[s, hal:hal + tm, :] = cur_ref[:, ls].astype(F32)
        zs_ref[s, hal + tm:hal + tm + hal, :] = next_ref[:, ls].astype(F32) * nscale

    def strip(s, carry):
        acc = jnp.zeros((tm, LANES), F32) + bdw_ref[s]
        for k in range(CONV_K):
            acc = acc + wdw_ref[s, k:k + 1, :] * zs_ref[s, hal - pad + k:hal - pad + k + tm, :]
        cz_ref[s] = acc
        return carry

    lax.fori_loop(0, n_strips, strip, 0)

    tot = jnp.zeros((tm, 1), F32)
    for s in range(n_strips):
        tot = tot + jnp.sum(cz_ref[s], axis=-1, keepdims=True)
    mu = tot / (n_strips * LANES)
    sq = jnp.zeros((tm, 1), F32)
    for s in range(n_strips):
        dlt = cz_ref[s] - mu
        sq = sq + jnp.sum(dlt * dlt, axis=-1, keepdims=True)
    rstd = lax.rsqrt(sq / (n_strips * LANES) + EPS)
    for s in range(n_strips):
        ls = slice(s * LANES, (s + 1) * LANES)
        y = (cz_ref[s] - mu) * rstd * lng_ref[:, ls] + lnb_ref[:, ls]
        h_ref[:, ls] = _silu(y).astype(BF16)
    o_ref[...] = _dot(h_ref[...], w_ref[...])
    _gated_residual_rows(x_ref, o_ref, mod_ref, g_ref, rstd_ref, 1, 1.0)


def _conv_mix(cfg, z, w_dw, b_dw, ln_g, ln_b, w_pw2, t, mods, g6):
    n_tok, d = t.shape
    tm = CONV_TM
    assert cfg.ctx == tm
    hal = CONV_HALO
    n_strips = d // LANES
    hb = tm // hal
    n_hal = n_tok // hal
    row = _mod_row(cfg, tm)
    return pl.pallas_call(
        functools.partial(_conv_body, lat_tiles=cfg.n_lat // tm, per_seq=cfg.seq // tm),
        grid=(n_tok // tm,),
        in_specs=[
            pl.BlockSpec((hal, d), lambda i: (jnp.maximum(i * hb - 1, 0), 0)),
            pl.BlockSpec((tm, d), lambda i: (i, 0)),
            pl.BlockSpec((hal, d), lambda i: (jnp.minimum((i + 1) * hb, n_hal - 1), 0)),
            pl.BlockSpec((n_strips, 32, LANES), lambda i: (0, 0, 0)),
            pl.BlockSpec((n_strips, 1, LANES), lambda i: (0, 0, 0)),
            pl.BlockSpec((1, d), lambda i: (0, 0)),
            pl.BlockSpec((1, d), lambda i: (0, 0)),
            pl.BlockSpec((d, d), lambda i: (0, 0)),
            pl.BlockSpec((tm, d), lambda i: (i, 0)),
            pl.BlockSpec((None, 9, d), lambda i: (row(i), 0, 0)),
            pl.BlockSpec((6, d), lambda i: (0, 0)),
        ],
        out_specs=pl.BlockSpec((tm, d), lambda i: (i, 0)),
        out_shape=jax.ShapeDtypeStruct((n_tok, d), F32),
        scratch_shapes=[pltpu.VMEM((n_strips, tm + 2 * hal, LANES), F32),
                        pltpu.VMEM((n_strips, tm, LANES), F32),
                        pltpu.VMEM((tm, d), BF16), pltpu.VMEM((tm, 1), F32)],
        compiler_params=_params("arbitrary"),
        name="conv_mix",
    )(z, z, z, w_dw, b_dw, ln_g, ln_b, w_pw2, t, mods, g6)


def _rope_tables(cfg, n_ident):
    quarter = RET_DK // 4
    pos = jnp.arange(cfg.seq, dtype=jnp.int32)
    rows = (pos // GRID_W).astype(F32)
    cols = (pos % GRID_W).astype(F32)
    inv = ROPE_BASE ** (-jnp.arange(quarter, dtype=F32) / quarter)
    ar = rows[:, None] * inv[None, :]
    ac = cols[:, None] * inv[None, :]
    cos = jnp.concatenate([jnp.cos(ar), jnp.cos(ar), jnp.cos(ac), jnp.cos(ac)], axis=1)
    sin = jnp.concatenate([-jnp.sin(ar), jnp.sin(ar), -jnp.sin(ac), jnp.sin(ac)], axis=1)
    cos = jnp.concatenate([cos, jnp.ones((n_ident, RET_DK), F32)], axis=0)
    sin = jnp.concatenate([sin, jnp.zeros((n_ident, RET_DK), F32)], axis=0)
    return cos, sin


def _forward(cfg, x, c, ctx, c_ctx, ada_w, ada_b, norm_g, ffn_w_in, ffn_w_out, ret_w_in, ret_w_out,
             ret_decay_logit, ret_gn_g, gmlp_w_in, gmlp_ln_g, gmlp_ln_b, gmlp_w_s, gmlp_b_s, gmlp_w_out,
             conv_w_pw1, conv_w_dw, conv_b_dw, conv_ln_g, conv_ln_b, conv_w_pw2):
    d = x.shape[-1]
    depth = ada_w.shape[0]
    t = x.reshape(cfg.n_lat, d)
    t_ctx = ctx.reshape(cfg.n_ctx, d)
    cc = jnp.zeros((ADA_ROWS, d), F32).at[:cfg.batch].set(c).at[cfg.batch].set(c_ctx)
    mods_all = _ada_table(cc, ada_w, ada_b)
    ret_w_in_bf = ret_w_in.astype(BF16)
    ret_w_out_bf = ret_w_out.astype(BF16)
    rope = _rope_tables(cfg, cfg.n_ctx)

    for i in range(depth):
        kind = i % N_MIXERS
        inst = i // N_MIXERS
        last = i == depth - 1
        mods = mods_all[i]
        g6 = norm_g[i].reshape(6, d)
        n_after = cfg.n_lat if last else cfg.n_tok
        t = _ffn_sublayer(cfg, t, mods, g6, ffn_w_in, ffn_w_out, i, 0, 0, cfg.n_tok,
                          t_ctx=t_ctx if i == 0 else None)
        if kind == 0:
            p, kt = _inproj_ret(cfg, t, mods, g6, ret_w_in_bf, inst, rope)
            log_g = jax.nn.log_sigmoid(ret_decay_logit[inst].astype(F32))
            dec = jnp.zeros((RET_HEADS, 8, LANES), F32).at[:, 0:2, :].set(
                jnp.broadcast_to(log_g.T[:, :, None], (RET_HEADS, 2, LANES)))
            a_lat, a_ctx = _ret_scan(cfg, p, kt, dec, ret_gn_g[inst].reshape(1, RET_V))
            t = _outproj(cfg, a_lat, a_ctx, ret_w_out_bf, inst, t, mods, g6, n_after)
        elif kind == 1:
            z = _inproj(cfg, "gelu", t, mods, g6, gmlp_w_in[inst])
            b_s = jnp.broadcast_to(gmlp_b_s[inst][:, :, None], (GMLP_GROUPS, CHUNK, LANES))
            t = _gmlp_mix(cfg, z, gmlp_ln_g[inst].reshape(1, GMLP_E), gmlp_ln_b[inst].reshape(1, GMLP_E),
                          gmlp_w_s[inst].astype(BF16), b_s, gmlp_w_out[inst].astype(BF16), t, mods, g6)
        else:
            z = _inproj(cfg, "glu", t, mods, g6, conv_w_pw1[inst].astype(BF16))
            n_strips = d // LANES
            w_dw = jnp.zeros((32, d), F32).at[:CONV_K].set(conv_w_dw[inst])
            w_dw = w_dw.reshape(32, n_strips, LANES).transpose(1, 0, 2)
            t = _conv_mix(cfg, z, w_dw, conv_b_dw[inst].reshape(n_strips, 1, LANES),
                          conv_ln_g[inst].reshape(1, d), conv_ln_b[inst].reshape(1, d),
                          conv_w_pw2[inst].astype(BF16), t, mods, g6)
        t = _ffn_sublayer(cfg, t, mods, g6, ffn_w_in, ffn_w_out, i, 1, 2, n_after)
    return t[:cfg.n_lat].reshape(x.shape)


def kernel(x, c, ctx, c_ctx, ada_w, ada_b, norm_g, ffn_w_in, ffn_w_out, ret_w_in, ret_w_out, ret_decay_logit,
           ret_gn_g, gmlp_w_in, gmlp_ln_g, gmlp_ln_b, gmlp_w_s, gmlp_b_s, gmlp_w_out, conv_w_pw1, conv_w_dw,
           conv_b_dw, conv_ln_g, conv_ln_b, conv_w_pw2):
    cfg = _Cfg(batch=x.shape[0], seq=x.shape[1], ctx=ctx.shape[1])
    return _forward(cfg, x, c, ctx, c_ctx, ada_w, ada_b, norm_g, ffn_w_in, ffn_w_out, ret_w_in, ret_w_out,
                    ret_decay_logit, ret_gn_g, gmlp_w_in, gmlp_ln_g, gmlp_ln_b, gmlp_w_s, gmlp_b_s, gmlp_w_out,
                    conv_w_pw1, conv_w_dw, conv_b_dw, conv_ln_g, conv_ln_b, conv_w_pw2)
```
